```python
import math
import jax
import jax.numpy as jnp
from jax import lax
import numpy as np

D_MODEL = 1024
BATCH = 2
SEQ = 8192
DEPTH = 4
DEC_BATCH = 128
DEC_SEQ = 4
PAST_LEN = 8192
PAGE_SIZE = 128

N_A_LAYERS = DEPTH // 2
N_B_LAYERS = DEPTH - N_A_LAYERS
MLA_HEADS = 16
Q_LORA = 384
KV_LORA = 256
QK_NOPE = 64
QK_ROPE = 32
V_DIM = 64
MLA_THETA = 10000.0
MLA_SCALE = (QK_NOPE + QK_ROPE) ** -0.5
MLA_ROW = KV_LORA + QK_ROPE
Q_BLOCK = 128
DIL_GROUPS = ((128, 1), (512, 4), (2048, 16))
N_GROUPS = len(DIL_GROUPS)
HEADS_PER_GROUP = 8
HEAD_DIM = 64
DIL_WIDTH = N_GROUPS * HEADS_PER_GROUP * HEAD_DIM
ROT_DIM = HEAD_DIM // 4
ROPE_THETA = 500000.0
DIL_SCALE = HEAD_DIM ** -0.5
D_FF = 2816
EPS = 1e-6
NEG = -1e30

kernel_name = 'hybrid_mla_dilated_yoco_decoder'


def rmsnorm(x, g):
    xf = x.astype(jnp.float32)
    y = xf * lax.rsqrt(jnp.mean(xf * xf, axis=-1, keepdims=True) + EPS)
    return (y * g.astype(jnp.float32)).astype(x.dtype)


def rope(x, pos, theta):
    half = x.shape[-1] // 2
    inv = jnp.float32(theta) ** (-2.0 * jnp.arange(half, dtype=jnp.float32) / x.shape[-1])
    ang = pos[:, None] * inv[None, :]
    cos = jnp.cos(ang)[:, None, :].astype(x.dtype)
    sin = jnp.sin(ang)[:, None, :].astype(x.dtype)
    x1, x2 = x[..., :half], x[..., half:]
    return jnp.concatenate([x1 * cos - x2 * sin, x1 * sin + x2 * cos], axis=-1)


def partial_rope(x, pos):
    return jnp.concatenate([rope(x[..., :ROT_DIM], pos, ROPE_THETA), x[..., ROT_DIM:]], axis=-1)


def swiglu(h, w_in, w_out):
    g, u = jnp.split(h @ w_in, 2, axis=-1)
    return (jax.nn.silu(g) * u) @ w_out


def softmax_lse(s):
    m = jnp.max(s, axis=-1, keepdims=True)
    e = jnp.exp(s - m)
    den = jnp.sum(e, axis=-1, keepdims=True)
    return e / den, (m + jnp.log(den))[..., 0]


def mla_project(h, pos, w_in, g_q, g_kv, w_qb):
    comb = h @ w_in
    cq = rmsnorm(comb[..., :Q_LORA], g_q)
    ckv = rmsnorm(comb[..., Q_LORA:Q_LORA + KV_LORA], g_kv)
    kr = rope(comb[..., None, Q_LORA + KV_LORA:], pos, MLA_THETA)[..., 0, :]
    q = (cq @ w_qb).reshape(h.shape[0], h.shape[1], MLA_HEADS, QK_NOPE + QK_ROPE)
    q_rope = rope(q[..., QK_NOPE:], pos, MLA_THETA)
    return q[..., :QK_NOPE], q_rope, ckv, kr


def mla_prompt(h, pos, w_in, g_q, g_kv, w_qb, w_ukv, w_o):
    B, S, _ = h.shape
    q_nope, q_rope, ckv, kr = mla_project(h, pos, w_in, g_q, g_kv, w_qb)
    kv = (ckv @ w_ukv).reshape(B, S, MLA_HEADS, QK_NOPE + V_DIM)
    k = jnp.concatenate([kv[..., :QK_NOPE], jnp.broadcast_to(kr[:, :, None, :], (B, S, MLA_HEADS, QK_ROPE))], axis=-1)
    v = kv[..., QK_NOPE:]
    q = jnp.concatenate([q_nope, q_rope], axis=-1)
    nb = S // Q_BLOCK
    qb = q.reshape(B, nb, Q_BLOCK, MLA_HEADS, QK_NOPE + QK_ROPE).transpose(1, 0, 2, 3, 4)
    kpos = jnp.arange(S)

    def block(args):
        qi, bi = args
        s = jnp.einsum('bqhd,bkhd->bhqk', qi, k).astype(jnp.float32) * MLA_SCALE
        qpos = bi * Q_BLOCK + jnp.arange(Q_BLOCK)
        s = jnp.where(kpos[None, :] <= qpos[:, None], s, NEG)
        p = jax.nn.softmax(s, axis=-1).astype(v.dtype)
        return jnp.einsum('bhqk,bkhd->bqhd', p, v)

    o = lax.map(block, (qb, jnp.arange(nb)))
    o = o.transpose(1, 0, 2, 3, 4).reshape(B, S, MLA_HEADS * V_DIM)
    return o @ w_o, jnp.concatenate([ckv, kr], axis=-1)


def mla_sample(h, pos, cache_l, page_table, w_in, g_q, g_kv, w_qb, w_ukv, w_o):
    Bd, T, _ = h.shape
    q_nope, q_rope, ckv, kr = mla_project(h, pos, w_in, g_q, g_kv, w_qb)
    past = cache_l[page_table].reshape(Bd, -1, MLA_ROW)
    rows_new = jnp.concatenate([ckv, kr], axis=-1).astype(past.dtype)
    w = w_ukv.reshape(KV_LORA, MLA_HEADS, QK_NOPE + V_DIM)
    q_lat = jnp.einsum('bthn,chn->bthc', q_nope, w[..., :QK_NOPE])
    qf = jnp.concatenate([q_lat, q_rope], axis=-1)
    s_past = jnp.einsum('bthc,bsc->bhts', qf, past).astype(jnp.float32)
    s_new = jnp.einsum('bthc,bsc->bhts', qf, rows_new).astype(jnp.float32)
    causal = jnp.arange(T)[None, :] <= jnp.arange(T)[:, None]
    s_new = jnp.where(causal, s_new, NEG)
    p = jax.nn.softmax(jnp.concatenate([s_past, s_new], axis=-1) * MLA_SCALE, axis=-1).astype(h.dtype)
    L = past.shape[1]
    o_lat = (jnp.einsum('bhts,bsc->bthc', p[..., :L], past[..., :KV_LORA])
             + jnp.einsum('bhts,bsc->bthc', p[..., L:], rows_new[..., :KV_LORA]))
    o = jnp.einsum('bthc,chv->bthv', o_lat, w[..., QK_NOPE:]).reshape(Bd, T, MLA_HEADS * V_DIM)
    return o @ w_o, rows_new


def shared_kv_groups(x, pos, g, w):
    h = rmsnorm(x, g)
    kv = (h @ w).reshape(x.shape[0], x.shape[1], 2, N_GROUPS, HEADS_PER_GROUP, HEAD_DIM)
    return [(partial_rope(kv[:, :, 0, gi], pos), kv[:, :, 1, gi]) for gi in range(N_GROUPS)]


def dil_queries(h, pos, w_q):
    q = (h @ w_q).reshape(h.shape[0], h.shape[1], N_GROUPS, HEADS_PER_GROUP, HEAD_DIM)
    return [partial_rope(q[:, :, gi], pos) for gi in range(N_GROUPS)]


def dil_blocks(t, window, dil):
    B, S, H, D = t.shape
    n = window // dil
    span = n * dil
    s_pad = -(-S // span) * span
    t = jnp.pad(t, ((0, 0), (0, s_pad - S), (0, 0), (0, 0)))
    return t.reshape(B, s_pad // span, n, dil, H, D)


def dil_with_prev(t):
    prev = jnp.pad(t, ((0, 0), (1, 0), (0, 0), (0, 0), (0, 0), (0, 0)))[:, :-1]
    return jnp.concatenate([prev, t], axis=2)


def dil_attend_prompt(q, kk, vv, window, dil):
    B, S, H, D = q.shape
    n = window // dil
    qb = dil_blocks(q, window, dil)
    NB = qb.shape[1]
    s = jnp.einsum('bnirhd,bnjrhd->bnrhij', qb, kk).astype(jnp.float32) * DIL_SCALE
    i = jnp.arange(n)[:, None]
    j = jnp.arange(2 * n)[None, :]
    delta = n + i - j
    blk = jnp.arange(NB)[:, None, None]
    valid = (delta >= 0) & (delta <= n) & ((blk > 0) | (j >= n))
    s = jnp.where(valid[None, :, None, None], s, NEG)
    p, lse = softmax_lse(s)
    o = jnp.einsum('bnrhij,bnjrhd->bnirhd', p.astype(vv.dtype), vv).reshape(B, -1, H, D)[:, :S]
    lse = lse.transpose(0, 1, 4, 2, 3).reshape(B, -1, H)[:, :S]
    return o, lse


def dil_attend_sample(q, kvg, valid):
    s = jnp.einsum('bthd,btkhd->bhtk', q, kvg[:, :, :, 0]).astype(jnp.float32) * DIL_SCALE
    s = jnp.where(valid[None, None], s, NEG)
    p, lse = softmax_lse(s)
    o = jnp.einsum('bhtk,btkhd->bthd', p.astype(kvg.dtype), kvg[:, :, :, 1])
    return o, lse.transpose(0, 2, 1)


def combine_groups(outs, lses, w_o):
    wts = jax.nn.softmax(jnp.stack(lses, axis=0), axis=0).astype(outs[0].dtype)
    o = jnp.einsum('gbth,gbthd->bthd', wts, jnp.stack(outs, axis=0))
    return o.reshape(o.shape[0], o.shape[1], HEADS_PER_GROUP * HEAD_DIM) @ w_o


def setup_inputs(seed: int = 0) -> dict:
    key = jax.random.key(seed)
    ks = jax.random.split(key, 24)
    f32 = jnp.float32

    def nrm(k, shape, scale):
        return jax.random.normal(k, shape, f32) * scale

    def gain(k, shape):
        return 1.0 + 0.01 * jax.random.normal(k, shape, f32)

    n_pages = PAST_LEN // PAGE_SIZE
    n_used = DEC_BATCH * n_pages
    n_phys = n_used + max(1, n_used // 4)
    page_table = jax.random.permutation(ks[3], n_phys)[:n_used].reshape(DEC_BATCH, n_pages).astype(jnp.int32)
    return {
        'x_prompt': nrm(ks[0], (BATCH, SEQ, D_MODEL), 1.0),
        'x_sample': nrm(ks[1], (DEC_BATCH, DEC_SEQ, D_MODEL), 1.0),
        'cache_mla': nrm(ks[2], (N_A_LAYERS, n_phys, PAGE_SIZE, MLA_ROW), 1.0),
        'page_table': page_table,
        'state_dil_w128': nrm(ks[4], (DEC_BATCH, min(DIL_GROUPS[0][0], PAST_LEN), 2, HEADS_PER_GROUP, HEAD_DIM), 1.0),
        'state_dil_w512': nrm(ks[5], (DEC_BATCH, min(DIL_GROUPS[1][0], PAST_LEN), 2, HEADS_PER_GROUP, HEAD_DIM), 1.0),
        'state_dil_w2048': nrm(ks[6], (DEC_BATCH, min(DIL_GROUPS[2][0], PAST_LEN), 2, HEADS_PER_GROUP, HEAD_DIM), 1.0),
        'g_layers': gain(ks[7], (DEPTH, 3, D_MODEL)),
        'w_ffn_in': nrm(ks[8], (DEPTH, 2, D_MODEL, 2 * D_FF), D_MODEL ** -0.5),
        'w_ffn_out': nrm(ks[9], (DEPTH, 2, D_FF, D_MODEL), D_FF ** -0.5),
        'w_mla_in': nrm(ks[10], (N_A_LAYERS, D_MODEL, Q_LORA + KV_LORA + QK_ROPE), D_MODEL ** -0.5),
        'g_mla_q': gain(ks[11], (N_A_LAYERS, Q_LORA)),
        'g_mla_kv': gain(ks[12], (N_A_LAYERS, KV_LORA)),
        'w_mla_qb': nrm(ks[13], (N_A_LAYERS, Q_LORA, MLA_HEADS * (QK_NOPE + QK_ROPE)), Q_LORA ** -0.5),
        'w_mla_ukv': nrm(ks[14], (N_A_LAYERS, KV_LORA, MLA_HEADS * (QK_NOPE + V_DIM)), KV_LORA ** -0.5),
        'w_mla_o': nrm(ks[15], (N_A_LAYERS, MLA_HEADS * V_DIM, D_MODEL), (MLA_HEADS * V_DIM) ** -0.5),
        'g_shared_kv': gain(ks[16], (D_MODEL,)),
        'w_shared_kv': nrm(ks[17], (D_MODEL, 2 * DIL_WIDTH), D_MODEL ** -0.5),
        'w_dil_q': nrm(ks[18], (N_B_LAYERS, D_MODEL, DIL_WIDTH), D_MODEL ** -0.5),
        'w_dil_o': nrm(ks[19], (N_B_LAYERS, HEADS_PER_GROUP * HEAD_DIM, D_MODEL), (HEADS_PER_GROUP * HEAD_DIM) ** -0.5),
        'g_final': gain(ks[20], (D_MODEL,)),
    }


def reference(x_prompt, x_sample, cache_mla, page_table, state_dil_w128, state_dil_w512, state_dil_w2048,
              g_layers, w_ffn_in, w_ffn_out, w_mla_in, g_mla_q, g_mla_kv, w_mla_qb, w_mla_ukv, w_mla_o,
              g_shared_kv, w_shared_kv, w_dil_q, w_dil_o, g_final):

    def trunk(x, pos, mix_a, prep_b, mix_b):
        rows_a = []
        shared = None
        win_states = None
        for l in range(DEPTH):
            x = x + 0.5 * swiglu(rmsnorm(x, g_layers[l, 0]), w_ffn_in[l, 0], w_ffn_out[l, 0])
            h = rmsnorm(x, g_layers[l, 1])
            if l < N_A_LAYERS:
                out, rows = mix_a(h, l)
                rows_a.append(rows)
            else:
                out = mix_b(h, l - N_A_LAYERS, shared)
            x = x + out
            x = x + 0.5 * swiglu(rmsnorm(x, g_layers[l, 2]), w_ffn_in[l, 1], w_ffn_out[l, 1])
            if l == N_A_LAYERS - 1:
                shared, win_states = prep_b(shared_kv_groups(x, pos, g_shared_kv, w_shared_kv))
        return rmsnorm(x, g_final), jnp.stack(rows_a, axis=0), win_states

    S = x_prompt.shape[1]
    pos_p = jnp.arange(S, dtype=jnp.float32)

    def mix_a_p(h, a):
        return mla_prompt(h, pos_p, w_mla_in[a], g_mla_q[a], g_mla_kv[a], w_mla_qb[a], w_mla_ukv[a], w_mla_o[a])

    def prep_b_p(kv):
        shared, states = [], []
        for (k, v), (win, dil) in zip(kv, DIL_GROUPS):
            shared.append((dil_with_prev(dil_blocks(k, win, dil)), dil_with_prev(dil_blocks(v, win, dil))))
            keep = min(win, S)
            states.append(jnp.stack([k, v], axis=2)[:, S - keep:])
        return shared, states

    def mix_b_p(h, b, shared):
        qs = dil_queries(h, pos_p, w_dil_q[b])
        res = [dil_attend_prompt(q, kk, vv, win, dil) for q, (kk, vv), (win, dil) in zip(qs, shared, DIL_GROUPS)]
        return combine_groups([r[0] for r in res], [r[1] for r in res], w_dil_o[b])

    y_prompt, mla_rows_prompt, dil_p = trunk(x_prompt, pos_p, mix_a_p, prep_b_p, mix_b_p)

    T = x_sample.shape[1]
    past_len = page_table.shape[1] * PAGE_SIZE
    pos_s = (past_len + jnp.arange(T)).astype(jnp.float32)
    buffers = [state_dil_w128, state_dil_w512, state_dil_w2048]

    def mix_a_s(h, a):
        return mla_sample(h, pos_s, cache_mla[a], page_table, w_mla_in[a], g_mla_q[a], g_mla_kv[a],
                          w_mla_qb[a], w_mla_ukv[a], w_mla_o[a])

    def prep_b_s(kv):
        shared, states = [], []
        for (k, v), buf, (win, dil) in zip(kv, buffers, DIL_GROUPS):
            L = buf.shape[1]
            kv_all = jnp.concatenate([buf, jnp.stack([k, v], axis=2).astype(buf.dtype)], axis=1)
            n = win // dil
            idx = L + jnp.arange(T)[:, None] - dil * jnp.arange(n + 1)[None, :]
            shared.append((kv_all[:, jnp.maximum(idx, 0)], idx >= 0))
            keep = min(win, L + T)
            states.append(kv_all[:, L + T - keep:])
        return shared, states

    def mix_b_s(h, b, shared):
        qs = dil_queries(h, pos_s, w_dil_q[b])
        res = [dil_attend_sample(q, kvg, valid) for q, (kvg, valid) in zip(qs, shared)]
        return combine_groups([r[0] for r in res], [r[1] for r in res], w_dil_o[b])

    y_sample, mla_rows_sample, dil_s = trunk(x_sample, pos_s, mix_a_s, prep_b_s, mix_b_s)

    dil_w128_prompt, dil_w512_prompt, dil_w2048_prompt = dil_p
    dil_w128_sample, dil_w512_sample, dil_w2048_sample = dil_s
    return (y_prompt, y_sample, mla_rows_prompt, mla_rows_sample,
            dil_w128_prompt, dil_w512_prompt, dil_w2048_prompt,
            dil_w128_sample, dil_w512_sample, dil_w2048_sample)
```

```python
import functools
import math

import numpy as np
import jax
import jax.numpy as jnp
from jax import lax
from jax.experimental import pallas as pl
from jax.experimental.pallas import tpu as pltpu

F32 = jnp.float32
BF16 = jnp.bfloat16

MLA_HEADS = 16
QK_NOPE = 64
QK_ROPE = 32
V_DIM = 64
MLA_THETA = 10000.0
MLA_SCALE = (QK_NOPE + QK_ROPE) ** -0.5
DIL_GROUPS = ((128, 1), (512, 4), (2048, 16))
N_GROUPS = len(DIL_GROUPS)
HEADS_PER_GROUP = 8
HEAD_DIM = 64
GROUP_WIDTH = HEADS_PER_GROUP * HEAD_DIM
DIL_WIDTH = N_GROUPS * GROUP_WIDTH
ROT_DIM = HEAD_DIM // 4
ROPE_THETA = 500000.0
DIL_SCALE = HEAD_DIM ** -0.5
DIL_N = 128
EPS = 1e-6
NEG = -1e30

LANES = 128
HEAD_BLOCK = 128
VMEM_LIMIT_BYTES = 56 * 1024 * 1024


def _cparams(*sem):
    return pltpu.CompilerParams(dimension_semantics=sem, vmem_limit_bytes=VMEM_LIMIT_BYTES)


def _row_tile(m, pref):
    t = min(m, pref)
    assert m % t == 0, (m, t)
    return t


def _rms(x, g):
    return x * lax.rsqrt(jnp.mean(x * x, axis=-1, keepdims=True) + EPS) * g


def _ffn_kernel(x_ref, g_ref, wg_ref, wu_ref, wo_ref, gf_ref, o_ref, h_sc, *, n_chunks, final_norm):
    c = pl.program_id(1)

    @pl.when(c == 0)
    def _():
        h_sc[...] = _rms(x_ref[...], g_ref[...]).astype(BF16)

    h = h_sc[...]
    gate = jnp.dot(h, wg_ref[...], preferred_element_type=F32)
    up = jnp.dot(h, wu_ref[...], preferred_element_type=F32)
    act = (gate / (1.0 + jnp.exp(-gate)) * up).astype(BF16)
    part = jnp.dot(act, wo_ref[...], preferred_element_type=F32)

    @pl.when(c == 0)
    def _():
        o_ref[...] = part

    @pl.when(c > 0)
    def _():
        o_ref[...] += part

    @pl.when(c == n_chunks - 1)
    def _():
        y = x_ref[...] + 0.5 * o_ref[...]
        if final_norm:
            y = _rms(y, gf_ref[...])
        o_ref[...] = y


def _ffn(x, g, w_in, w_out, g_final, *, final_norm=False):
    m, d = x.shape
    d_ff = w_out.shape[0]
    tm = _row_tile(m, 512)
    n_chunks = 2 if d_ff % (2 * LANES) == 0 else 1
    tf = d_ff // n_chunks
    kern = functools.partial(_ffn_kernel, n_chunks=n_chunks, final_norm=final_norm)
    return pl.pallas_call(
        kern,
        grid=(m // tm, n_chunks),
        in_specs=[
            pl.BlockSpec((tm, d), lambda i, c: (i, 0)),
            pl.BlockSpec((1, d), lambda i, c: (0, 0)),
            pl.BlockSpec((d, tf), lambda i, c: (0, c)),
            pl.BlockSpec((d, tf), lambda i, c: (0, n_chunks + c)),
            pl.BlockSpec((tf, d), lambda i, c: (c, 0)),
            pl.BlockSpec((1, d), lambda i, c: (0, 0)),
        ],
        out_specs=pl.BlockSpec((tm, d), lambda i, c: (i, 0)),
        out_shape=jax.ShapeDtypeStruct((m, d), F32),
        scratch_shapes=[pltpu.VMEM((tm, d), BF16)],
        compiler_params=_cparams("parallel", "arbitrary"),
        name="ffn",
    )(x, g.reshape(1, d), w_in, w_in, w_out, g_final.reshape(1, d))


def _linear_kernel(*refs, has_res):
    if has_res:
        a_ref, w_ref, r_ref, o_ref = refs
    else:
        a_ref, w_ref, o_ref = refs
    acc = jnp.dot(a_ref[...], w_ref[...], preferred_element_type=F32)
    if has_res:
        acc = acc + r_ref[...]
    o_ref[...] = acc.astype(o_ref.dtype)


def _linear(a, w, residual=None, out_dtype=F32):
    m, k = a.shape
    n = w.shape[1]
    tm = _row_tile(m, 512)
    tn = _row_tile(n, 1024)
    in_specs = [pl.BlockSpec((tm, k), lambda i, j: (i, 0)), pl.BlockSpec((k, tn), lambda i, j: (0, j))]
    args = [a, w]
    if residual is not None:
        in_specs.append(pl.BlockSpec((tm, tn), lambda i, j: (i, j)))
        args.append(residual)
    return pl.pallas_call(
        functools.partial(_linear_kernel, has_res=residual is not None),
        grid=(m // tm, n // tn),
        in_specs=in_specs,
        out_specs=pl.BlockSpec((tm, tn), lambda i, j: (i, j)),
        out_shape=jax.ShapeDtypeStruct((m, n), out_dtype),
        compiler_params=_cparams("parallel", "parallel"),
        name="linear",
    )(*args)


def _mla_proj_kernel(x_ref, g1_ref, win_ref, gq_ref, gkv_ref, wqb_ref, wukv_ref,
                     cq_ref, sq_ref, ck_ref, sk_ref, vone_ref,
                     q_ref, ckv_ref, kr_ref, *kv_refs, q_lora, kv_lora, with_kv):
    h = _rms(x_ref[...], g1_ref[...]).astype(BF16)
    comb = jnp.dot(h, win_ref[...], preferred_element_type=F32)
    cq = _rms(comb[:, :q_lora], gq_ref[...])
    ckv = _rms(comb[:, q_lora:q_lora + kv_lora], gkv_ref[...])
    blk = comb[:, q_lora + kv_lora:]
    kr = blk * ck_ref[...] + pltpu.roll(blk, HEAD_BLOCK - QK_ROPE, 1) * sk_ref[...]
    ckv_ref[...] = ckv
    kr_ref[...] = kr

    q = jnp.dot(cq.astype(BF16), wqb_ref[...], preferred_element_type=F32)
    cq_t = cq_ref[...]
    sq_t = sq_ref[...]
    for hd in range(MLA_HEADS):
        sl = slice(hd * HEAD_BLOCK, (hd + 1) * HEAD_BLOCK)
        qb = q[:, sl]
        q_ref[:, sl] = (qb * cq_t + pltpu.roll(qb, HEAD_BLOCK - QK_ROPE, 1) * sq_t).astype(BF16)

    if with_kv:
        k_ref, v_ref = kv_refs
        kv = jnp.dot(ckv.astype(BF16), wukv_ref[...], preferred_element_type=F32)
        width = MLA_HEADS * HEAD_BLOCK
        for hd in range(MLA_HEADS):
            sl = slice(hd * HEAD_BLOCK, (hd + 1) * HEAD_BLOCK)
            k_ref[:, sl] = (kv[:, sl] + kr).astype(BF16)
        v_ref[...] = (kv[:, width:] + vone_ref[...]).astype(BF16)


def _mla_proj(x, g1, w_in_ext, g_q, g_kv, w_qb_ext, w_ukv_ext, tabs, v_one, with_kv):
    m, d = x.shape
    q_lora = g_q.shape[0]
    kv_lora = g_kv.shape[0]
    tm = _row_tile(m, 512)
    width = MLA_HEADS * HEAD_BLOCK
    n_pos = tabs[0].shape[0] // tm
    full = lambda a: pl.BlockSpec(a.shape, lambda i: (0,) * a.ndim)
    tab_spec = pl.BlockSpec((tm, LANES), lambda i: (i % n_pos, 0))
    row = lambda n: pl.BlockSpec((tm, n), lambda i: (i, 0))
    out_shape = [jax.ShapeDtypeStruct((m, width), BF16), jax.ShapeDtypeStruct((m, kv_lora), F32),
                 jax.ShapeDtypeStruct((m, LANES), F32)]
    out_specs = [row(width), row(kv_lora), row(LANES)]
    if with_kv:
        out_shape += [jax.ShapeDtypeStruct((m, width), BF16), jax.ShapeDtypeStruct((m, width), BF16)]
        out_specs += [row(width), row(width)]
    g1 = g1.reshape(1, d)
    g_q = g_q.reshape(1, q_lora)
    g_kv = g_kv.reshape(1, kv_lora)
    return pl.pallas_call(
        functools.partial(_mla_proj_kernel, q_lora=q_lora, kv_lora=kv_lora, with_kv=with_kv),
        grid=(m // tm,),
        in_specs=[row(d), full(g1), full(w_in_ext), full(g_q), full(g_kv), full(w_qb_ext), full(w_ukv_ext),
                  tab_spec, tab_spec, tab_spec, tab_spec, full(v_one)],
        out_specs=out_specs,
        out_shape=out_shape,
        compiler_params=_cparams("parallel"),
        name="mla_proj",
    )(x, g1, w_in_ext, g_q, g_kv, w_qb_ext, w_ukv_ext, *tabs, v_one)


def _mla_flash_kernel(q_ref, k_ref, v_ref, o_ref, m_sc, acc_sc, *, tq, tk):
    qi = pl.program_id(2)
    n_diag = tq // tk
    n_full = qi * n_diag
    lane = lax.broadcasted_iota(jnp.int32, (tq, HEAD_BLOCK), 1)
    outs = []
    for hh in range(2):
        hs = slice(hh * HEAD_BLOCK, (hh + 1) * HEAD_BLOCK)
        q = q_ref[:, hs]
        m_sc[...] = jnp.full(m_sc.shape, NEG, F32)
        acc_sc[...] = jnp.zeros(acc_sc.shape, F32)

        def step(kc, masked, q=q, hs=hs):
            start = pl.multiple_of(kc * tk, tk)
            k = k_ref[pl.ds(start, tk), hs]
            v = v_ref[pl.ds(start, tk), hs]
            s = lax.dot_general(q, k, (((1,), (1,)), ((), ())), preferred_element_type=F32)
            if masked:
                row = qi * tq + lax.broadcasted_iota(jnp.int32, (tq, tk), 0)
                col = kc * tk + lax.broadcasted_iota(jnp.int32, (tq, tk), 1)
                s = jnp.where(col <= row, s, NEG)
            m_prev = m_sc[...]
            m_new = jnp.maximum(m_prev, jnp.max(s, axis=-1, keepdims=True))
            alpha = jnp.exp(m_prev - m_new)
            p = jnp.exp(s - pltpu.repeat(m_new, tk // LANES, axis=1))
            acc_sc[...] = acc_sc[...] * alpha + jnp.dot(p.astype(BF16), v, preferred_element_type=F32)
            m_sc[...] = m_new

        def body(kc, carry):
            step(kc, False)
            return carry

        lax.fori_loop(0, n_full, body, 0)
        for j in range(n_diag):
            step(n_full + j, True)
        acc = acc_sc[...]
        one_lane = V_DIM if hh == 0 else 0
        outs.append(acc / acc[:, one_lane:one_lane + 1])
    o_ref[...] = jnp.where(lane < V_DIM, outs[0], outs[1]).astype(o_ref.dtype)


def _mla_flash(q, k, v, batch, seq):
    width = MLA_HEADS * HEAD_BLOCK
    q = q.reshape(batch, seq, width)
    k = k.reshape(batch, seq, width)
    v = v.reshape(batch, seq, width)
    tq = _row_tile(seq, 512)
    tk = tq
    pair = 2 * HEAD_BLOCK
    out = pl.pallas_call(
        functools.partial(_mla_flash_kernel, tq=tq, tk=tk),
        grid=(batch, MLA_HEADS // 2, seq // tq),
        in_specs=[
            pl.BlockSpec((None, tq, pair), lambda b, h, i: (b, i, h)),
            pl.BlockSpec((None, seq, pair), lambda b, h, i: (b, 0, h)),
            pl.BlockSpec((None, seq, pair), lambda b, h, i: (b, 0, h)),
        ],
        out_specs=pl.BlockSpec((None, tq, 2 * V_DIM), lambda b, h, i: (b, i, h)),
        out_shape=jax.ShapeDtypeStruct((batch, seq, MLA_HEADS * V_DIM), BF16),
        scratch_shapes=[pltpu.VMEM((tq, HEAD_BLOCK), F32), pltpu.VMEM((tq, HEAD_BLOCK), F32)],
        compiler_params=_cparams("parallel", "parallel", "arbitrary"),
        name="mla_flash",
    )(q, k, v)
    return out.reshape(batch * seq, MLA_HEADS * V_DIM)


def _mla_decode_kernel(pt_ref, ql_ref, qr_ref, new_ref, *rest, n_pages_step, n_steps, page, kv_lora, t_new):
    page_refs = rest[:n_pages_step]
    o_ref = rest[n_pages_step]
    lat_sc, kr_sc, m_sc, l_sc, acc_sc = rest[n_pages_step + 1:]
    c = pl.program_id(1)
    rows = ql_ref.shape[0]

    @pl.when(c == 0)
    def _():
        m_sc[...] = jnp.full(m_sc.shape, NEG, F32)
        l_sc[...] = jnp.zeros(l_sc.shape, F32)
        acc_sc[...] = jnp.zeros(acc_sc.shape, F32)

    for i, ref in enumerate(list(page_refs) + [new_ref]):
        lat_sc[i * page:(i + 1) * page, :] = ref[:, :kv_lora].astype(BF16)
        kr_sc[i * page:(i + 1) * page, :] = ref[:, kv_lora:].astype(BF16)

    dn = (((1,), (1,)), ((), ()))
    s = (lax.dot_general(ql_ref[...], lat_sc[...], dn, preferred_element_type=F32)
         + lax.dot_general(qr_ref[...], kr_sc[...], dn, preferred_element_type=F32))
    n_keys = (n_pages_step + 1) * page
    col = lax.broadcasted_iota(jnp.int32, (rows, n_keys), 1)
    t_row = lax.shift_right_logical(lax.broadcasted_iota(jnp.int32, (rows, n_keys), 0),
                                    int(math.log2(MLA_HEADS)))
    j_new = col - n_pages_step * page
    n_new = jnp.where(c == n_steps - 1, t_new, 0)
    valid = (j_new < 0) | ((j_new <= t_row) & (j_new < n_new))
    s = jnp.where(valid, s, NEG)

    m_prev = m_sc[...]
    m_new = jnp.maximum(m_prev, jnp.max(s, axis=-1, keepdims=True))
    alpha = jnp.exp(m_prev - m_new)
    p = jnp.exp(s - m_new[:, :1])
    l_sc[...] = l_sc[...] * alpha + jnp.sum(p, axis=-1, keepdims=True)
    acc_sc[...] = acc_sc[...] * alpha[:, :1] + jnp.dot(p.astype(BF16), lat_sc[...], preferred_element_type=F32)
    m_sc[...] = m_new

    @pl.when(c == n_steps - 1)
    def _():
        o_ref[...] = (acc_sc[...] / l_sc[...][:, :1]).astype(o_ref.dtype)


def _mla_decode(q_lat, q_rope, new_pad, cache_l, page_table):
    bd, rows, kv_lora = q_lat.shape
    n_pages = page_table.shape[1]
    page = cache_l.shape[1]
    row_w = cache_l.shape[2]
    n_pages_step = min(n_pages, 32)
    assert n_pages % n_pages_step == 0
    n_steps = n_pages // n_pages_step
    t_new = rows // MLA_HEADS
    n_keys = (n_pages_step + 1) * page

    def page_spec(i):
        return pl.BlockSpec((None, page, row_w), lambda b, c, pt: (pt[b, c * n_pages_step + i], 0, 0))

    grid_spec = pltpu.PrefetchScalarGridSpec(
        num_scalar_prefetch=1,
        grid=(bd, n_steps),
        in_specs=[
            pl.BlockSpec((None, rows, kv_lora), lambda b, c, pt: (b, 0, 0)),
            pl.BlockSpec((None, rows, QK_ROPE), lambda b, c, pt: (b, 0, 0)),
            pl.BlockSpec((None, page, row_w), lambda b, c, pt: (b, 0, 0)),
        ] + [page_spec(i) for i in range(n_pages_step)],
        out_specs=pl.BlockSpec((None, rows, kv_lora), lambda b, c, pt: (b, 0, 0)),
        scratch_shapes=[
            pltpu.VMEM((n_keys, kv_lora), BF16),
            pltpu.VMEM((n_keys, QK_ROPE), BF16),
            pltpu.VMEM((rows, LANES), F32),
            pltpu.VMEM((rows, LANES), F32),
            pltpu.VMEM((rows, kv_lora), F32),
        ],
    )
    kern = functools.partial(_mla_decode_kernel, n_pages_step=n_pages_step, n_steps=n_steps, page=page,
                             kv_lora=kv_lora, t_new=t_new)
    return pl.pallas_call(
        kern,
        grid_spec=grid_spec,
        out_shape=jax.ShapeDtypeStruct((bd, rows, kv_lora), BF16),
        compiler_params=_cparams("parallel", "arbitrary"),
        name="mla_decode",
    )(page_table, q_lat, q_rope, new_pad, *([cache_l] * n_pages_step))


def _norm_rope_kernel(x_ref, g_ref, w_ref, c_ref, s1_ref, s2_ref, *out_refs, rope_blocks, emit_f32):
    h = _rms(x_ref[...], g_ref[...]).astype(BF16)
    y = jnp.dot(h, w_ref[...], preferred_element_type=F32)
    ct, s1, s2 = c_ref[...], s1_ref[...], s2_ref[...]
    half = ROT_DIM // 2
    for j, roped in enumerate(rope_blocks):
        sl = slice(j * LANES, (j + 1) * LANES)
        blk = y[:, sl]
        if roped:
            blk = blk * ct + pltpu.roll(blk, LANES - half, 1) * s1 + pltpu.roll(blk, half, 1) * s2
        out_refs[0][:, sl] = blk.astype(BF16)
        if emit_f32:
            out_refs[1][:, sl] = blk


def _norm_rope(x, g, w, tabs, rope_blocks, emit_f32):
    m, d = x.shape
    n = w.shape[1]
    tm = _row_tile(m, 512)
    n_pos = tabs[0].shape[0] // tm
    tab_spec = pl.BlockSpec((tm, LANES), lambda i: (i % n_pos, 0))
    out_shape = [jax.ShapeDtypeStruct((m, n), BF16)]
    out_specs = [pl.BlockSpec((tm, n), lambda i: (i, 0))]
    if emit_f32:
        out_shape.append(jax.ShapeDtypeStruct((m, n), F32))
        out_specs.append(pl.BlockSpec((tm, n), lambda i: (i, 0)))
    return pl.pallas_call(
        functools.partial(_norm_rope_kernel, rope_blocks=tuple(rope_blocks), emit_f32=emit_f32),
        grid=(m // tm,),
        in_specs=[pl.BlockSpec((tm, d), lambda i: (i, 0)), pl.BlockSpec((1, d), lambda i: (0, 0)),
                  pl.BlockSpec((d, n), lambda i: (0, 0)), tab_spec, tab_spec, tab_spec],
        out_specs=out_specs,
        out_shape=out_shape,
        compiler_params=_cparams("parallel"),
        name="norm_rope",
    )(x, g.reshape(1, d), w, *tabs)


def _dil_prompt_kernel(q_ref, kp_ref, kc_ref, vp_ref, vc_ref, o_ref, lse_ref, *, tq):
    cblk = pl.program_id(2)
    lane = lax.broadcasted_iota(jnp.int32, (tq, LANES), 1)
    i_idx = lax.broadcasted_iota(jnp.int32, (tq, tq), 0)
    j_idx = lax.broadcasted_iota(jnp.int32, (tq, tq), 1)
    valid_prev = (j_idx >= i_idx) & (cblk > 0)
    valid_cur = j_idx <= i_idx
    dn = (((1,), (1,)), ((), ()))
    for hp in range(HEADS_PER_GROUP // 2):
        sl = slice(hp * LANES, (hp + 1) * LANES)
        q2 = q_ref[:, sl]
        kp, kc, vp, vc = kp_ref[:, sl], kc_ref[:, sl], vp_ref[:, sl], vc_ref[:, sl]
        o_h, lse_h = [], []
        for hh in range(2):
            head_lanes = (lane < HEAD_DIM) if hh == 0 else (lane >= HEAD_DIM)
            qm = jnp.where(head_lanes, q2, jnp.zeros_like(q2))
            sp = jnp.where(valid_prev, lax.dot_general(qm, kp, dn, preferred_element_type=F32), NEG)
            sc = jnp.where(valid_cur, lax.dot_general(qm, kc, dn, preferred_element_type=F32), NEG)
            m = jnp.maximum(jnp.max(sp, axis=-1, keepdims=True), jnp.max(sc, axis=-1, keepdims=True))
            pp = jnp.exp(sp - m)
            pc = jnp.exp(sc - m)
            den = jnp.sum(pp, axis=-1, keepdims=True) + jnp.sum(pc, axis=-1, keepdims=True)
            o = (jnp.dot(pp.astype(BF16), vp, preferred_element_type=F32)
                 + jnp.dot(pc.astype(BF16), vc, preferred_element_type=F32)) / den
            o_h.append(o)
            lse_h.append(jnp.broadcast_to(m + jnp.log(den), (tq, LANES)))
        o_ref[:, sl] = jnp.where(lane < HEAD_DIM, o_h[0], o_h[1])
        lse_ref[:, sl] = jnp.where(lane < HEAD_DIM, lse_h[0], lse_h[1])


def _dil_prompt_group(q, kv, gi, dil, batch, seq):
    assert seq % (dil * DIL_N) == 0
    rows = seq // dil
    tq = DIL_N
    nblk = rows // tq
    qv = q.reshape(batch, rows, dil * DIL_WIDTH)
    kvv = kv.reshape(batch, rows, dil * 2 * DIL_WIDTH)
    nq, nkv = N_GROUPS, 2 * N_GROUPS
    blk = (None, tq, GROUP_WIDTH)
    cur = lambda off: pl.BlockSpec(blk, lambda b, r, c: (b, c, r * nkv + 2 * gi + off))
    prev = lambda off: pl.BlockSpec(blk, lambda b, r, c: (b, jnp.maximum(c - 1, 0), r * nkv + 2 * gi + off))
    qspec = pl.BlockSpec(blk, lambda b, r, c: (b, c, r * nq + gi))
    o, lse = pl.pallas_call(
        functools.partial(_dil_prompt_kernel, tq=tq),
        grid=(batch, dil, nblk),
        in_specs=[qspec, prev(0), cur(0), prev(1), cur(1)],
        out_specs=[pl.BlockSpec(blk, lambda b, r, c: (b, c, r)), pl.BlockSpec(blk, lambda b, r, c: (b, c, r))],
        out_shape=[jax.ShapeDtypeStruct((batch, rows, dil * GROUP_WIDTH), F32)] * 2,
        compiler_params=_cparams("parallel", "parallel", "arbitrary"),
        name="dil_prompt",
    )(qv, kvv, kvv, kvv, kvv)
    return o.reshape(batch * seq, GROUP_WIDTH), lse.reshape(batch * seq, GROUP_WIDTH)


def _dil_sample_kernel(q_ref, buf_ref, new_ref, o_ref, lse_ref, *, bb, t_new, per_token_keys):
    lane_idx = lax.broadcasted_iota(jnp.int32, (HEADS_PER_GROUP, GROUP_WIDTH), 1)
    head_lo = lax.broadcasted_iota(jnp.int32, (HEADS_PER_GROUP, GROUP_WIDTH), 0) * HEAD_DIM
    diag = (lane_idx >= head_lo) & (lane_idx < head_lo + HEAD_DIM)
    key_idx = lax.broadcasted_iota(jnp.int32, (HEADS_PER_GROUP, DIL_N), 1)
    new_idx = lax.broadcasted_iota(jnp.int32, (HEADS_PER_GROUP, new_ref.shape[1]), 1)
    dn = (((1,), (1,)), ((), ()))
    kv_w = 2 * GROUP_WIDTH
    for b in range(bb):
        k_new = new_ref[b, :, :GROUP_WIDTH].astype(BF16)
        v_new = new_ref[b, :, GROUP_WIDTH:].astype(BF16)
        for t in range(t_new):
            off = t * kv_w if per_token_keys else 0
            k = buf_ref[b, :, off:off + GROUP_WIDTH].astype(BF16)
            v = buf_ref[b, :, off + GROUP_WIDTH:off + kv_w].astype(BF16)
            q_row = q_ref[b, t:t + 1, :]
            qbd = jnp.where(diag, jnp.broadcast_to(q_row, (HEADS_PER_GROUP, GROUP_WIDTH)), 0.0).astype(BF16)
            s = lax.dot_general(qbd, k, dn, preferred_element_type=F32)
            s_new = lax.dot_general(qbd, k_new, dn, preferred_element_type=F32)
            if per_token_keys:
                s_new = jnp.where(new_idx == t, s_new, NEG)
            else:
                s = jnp.where(key_idx >= t, s, NEG)
                s_new = jnp.where(new_idx <= t, s_new, NEG)
            m = jnp.maximum(jnp.max(s, axis=-1, keepdims=True), jnp.max(s_new, axis=-1, keepdims=True))
            p = jnp.exp(s - m)
            p_new = jnp.exp(s_new - m)
            den = jnp.sum(p, axis=-1, keepdims=True) + jnp.sum(p_new, axis=-1, keepdims=True)
            o8 = (jnp.dot(p.astype(BF16), v, preferred_element_type=F32)
                  + jnp.dot(p_new.astype(BF16), v_new, preferred_element_type=F32)) / den
            lse8 = jnp.broadcast_to(m + jnp.log(den), (HEADS_PER_GROUP, GROUP_WIDTH))
            o_ref[b, t:t + 1, :] = jnp.sum(jnp.where(diag, o8, 0.0), axis=0, keepdims=True)
            lse_ref[b, t:t + 1, :] = jnp.sum(jnp.where(diag, lse8, 0.0), axis=0, keepdims=True)


def _dil_sample_group(q_g, buf, new_g, dil):
    bd, t_new, _ = q_g.shape
    w = buf.shape[1]
    assert w == dil * DIL_N, "state buffer must hold exactly one window"
    kv_w = 2 * GROUP_WIDTH
    per_token_keys = dil > 1
    if per_token_keys:
        assert dil >= t_new
    bufv = buf.reshape(bd, DIL_N, dil * kv_w)
    lanes_needed = (t_new if per_token_keys else 1) * kv_w
    bb = _row_tile(bd, 2)
    o, lse = pl.pallas_call(
        functools.partial(_dil_sample_kernel, bb=bb, t_new=t_new, per_token_keys=per_token_keys),
        grid=(bd // bb,),
        in_specs=[pl.BlockSpec((bb, t_new, GROUP_WIDTH), lambda i: (i, 0, 0)),
                  pl.BlockSpec((bb, DIL_N, lanes_needed), lambda i: (i, 0, 0)),
                  pl.BlockSpec((bb,) + new_g.shape[1:], lambda i: (i, 0, 0))],
        out_specs=[pl.BlockSpec((bb, t_new, GROUP_WIDTH), lambda i: (i, 0, 0))] * 2,
        out_shape=[jax.ShapeDtypeStruct((bd, t_new, GROUP_WIDTH), F32)] * 2,
        compiler_params=_cparams("parallel"),
        name="dil_sample",
    )(q_g, bufv, new_g)
    return o.reshape(bd * t_new, GROUP_WIDTH), lse.reshape(bd * t_new, GROUP_WIDTH)


def _dil_out_kernel(o0_ref, o1_ref, o2_ref, l0_ref, l1_ref, l2_ref, w_ref, x_ref, y_ref):
    l0, l1, l2 = l0_ref[...], l1_ref[...], l2_ref[...]
    m = jnp.maximum(jnp.maximum(l0, l1), l2)
    e0, e1, e2 = jnp.exp(l0 - m), jnp.exp(l1 - m), jnp.exp(l2 - m)
    den = e0 + e1 + e2
    o = (e0 / den) * o0_ref[...] + (e1 / den) * o1_ref[...] + (e2 / den) * o2_ref[...]
    y_ref[...] = x_ref[...] + jnp.dot(o.astype(BF16), w_ref[...], preferred_element_type=F32)


def _dil_out(outs, lses, w_o, x):
    m, d = x.shape
    tm = _row_tile(m, 512)
    gspec = pl.BlockSpec((tm, GROUP_WIDTH), lambda i: (i, 0))
    return pl.pallas_call(
        _dil_out_kernel,
        grid=(m // tm,),
        in_specs=[gspec] * 6 + [pl.BlockSpec(w_o.shape, lambda i: (0, 0)), pl.BlockSpec((tm, d), lambda i: (i, 0))],
        out_specs=pl.BlockSpec((tm, d), lambda i: (i, 0)),
        out_shape=jax.ShapeDtypeStruct((m, d), F32),
        compiler_params=_cparams("parallel"),
        name="dil_out",
    )(*outs, *lses, w_o, x)


def _rope_cos_sin(pos, rot_dim, theta):
    half = rot_dim // 2
    inv = jnp.float32(theta) ** (-2.0 * jnp.arange(half, dtype=F32) / rot_dim)
    ang = pos[:, None] * inv[None, :]
    return jnp.cos(ang), jnp.sin(ang)


def _mla_tables(pos):
    cos, sin = _rope_cos_sin(pos, QK_ROPE, MLA_THETA)
    n = pos.shape[0]
    z = lambda w: jnp.zeros((n, w), F32)
    cos2 = jnp.concatenate([cos, cos], axis=1)
    sin2 = jnp.concatenate([sin, sin], axis=1)
    rest = HEAD_BLOCK - QK_NOPE - QK_ROPE
    cq = jnp.concatenate([jnp.ones((n, QK_NOPE), F32), cos2, z(rest)], axis=1) * MLA_SCALE
    sq = jnp.concatenate([z(QK_NOPE), sin2, z(rest)], axis=1) * MLA_SCALE
    ck = jnp.concatenate([z(QK_NOPE), cos2, z(rest)], axis=1)
    sk = jnp.concatenate([z(QK_NOPE), sin2, z(rest)], axis=1)
    return cq, sq, ck, sk


def _dil_tables(pos, scale):
    cos, sin = _rope_cos_sin(pos, ROT_DIM, ROPE_THETA)
    n = pos.shape[0]
    half = ROT_DIM // 2
    z = lambda w: jnp.zeros((n, w), F32)
    keep = jnp.ones((n, HEAD_DIM - ROT_DIM), F32)
    c_head = jnp.concatenate([cos, cos, keep], axis=1)
    s1_head = jnp.concatenate([-sin, z(HEAD_DIM - half)], axis=1)
    s2_head = jnp.concatenate([z(half), sin, z(HEAD_DIM - ROT_DIM)], axis=1)
    two = lambda a: jnp.concatenate([a, a], axis=1) * scale
    return two(c_head), two(s1_head), two(s2_head)


def _rot_half_cols(w):
    half = w.shape[-1] // 2
    return jnp.concatenate([-w[..., half:], w[..., :half]], axis=-1)


def _prep_mla_weights(w_in, w_qb, w_ukv, q_lora, kv_lora):
    d = w_in.shape[0]
    w_kr = w_in[:, q_lora + kv_lora:]
    rest = HEAD_BLOCK - QK_NOPE - QK_ROPE
    w_in_ext = jnp.concatenate(
        [w_in[:, :q_lora + kv_lora], jnp.zeros((d, QK_NOPE), F32), w_kr, _rot_half_cols(w_kr)], axis=1)
    wq = w_qb.reshape(q_lora, MLA_HEADS, QK_NOPE + QK_ROPE)
    wq_rope = wq[..., QK_NOPE:]
    w_qb_ext = jnp.concatenate([wq[..., :QK_NOPE], wq_rope, _rot_half_cols(wq_rope)], axis=-1)
    assert rest == QK_ROPE
    w_qb_ext = w_qb_ext.reshape(q_lora, MLA_HEADS * HEAD_BLOCK)
    wkv = w_ukv.reshape(kv_lora, MLA_HEADS, QK_NOPE + V_DIM)
    zpad = jnp.zeros((kv_lora, MLA_HEADS, HEAD_BLOCK - QK_NOPE), F32)
    wk = jnp.concatenate([wkv[..., :QK_NOPE], zpad], axis=-1).reshape(kv_lora, MLA_HEADS * HEAD_BLOCK)
    wv_even = jnp.concatenate([wkv[..., QK_NOPE:], zpad], axis=-1)
    wv_odd = jnp.concatenate([zpad, wkv[..., QK_NOPE:]], axis=-1)
    odd = (jnp.arange(MLA_HEADS) % 2 == 1)[None, :, None]
    wv = jnp.where(odd, wv_odd, wv_even).reshape(kv_lora, MLA_HEADS * HEAD_BLOCK)
    w_ukv_ext = jnp.concatenate([wk, wv], axis=1)
    one = np.zeros((1, MLA_HEADS, HEAD_BLOCK), np.float32)
    one[0, 0::2, V_DIM] = 1.0
    one[0, 1::2, 0] = 1.0
    v_one = jnp.asarray(one.reshape(1, MLA_HEADS * HEAD_BLOCK))
    eye = jnp.eye(MLA_HEADS, dtype=F32)
    w_uk = wkv[..., :QK_NOPE]
    w_uk_rows = jnp.concatenate([jnp.transpose(w_uk, (1, 2, 0)),
                                 jnp.zeros((MLA_HEADS, HEAD_BLOCK - QK_NOPE, kv_lora), F32)], axis=1)
    w_uk_bd = jnp.einsum('hnc,hg->hngc', w_uk_rows, eye).reshape(MLA_HEADS * HEAD_BLOCK, MLA_HEADS * kv_lora)
    w_uv = jnp.transpose(wkv[..., QK_NOPE:], (1, 0, 2))
    w_uv_bd = jnp.einsum('hcv,hg->hcgv', w_uv, eye).reshape(MLA_HEADS * kv_lora, MLA_HEADS * V_DIM)
    bf = lambda a: a.astype(BF16)
    return bf(w_in_ext), bf(w_qb_ext), bf(w_ukv_ext), v_one, bf(w_uk_bd), bf(w_uv_bd)


def _group_major_kv_cols(w_shared):
    d = w_shared.shape[0]
    w = w_shared.reshape(d, 2, N_GROUPS, GROUP_WIDTH)
    return jnp.transpose(w, (0, 2, 1, 3)).reshape(d, 2 * DIL_WIDTH)


def kernel(x_prompt, x_sample, cache_mla, page_table, state_dil_w128, state_dil_w512, state_dil_w2048,
           g_layers, w_ffn_in, w_ffn_out, w_mla_in, g_mla_q, g_mla_kv, w_mla_qb, w_mla_ukv, w_mla_o,
           g_shared_kv, w_shared_kv, w_dil_q, w_dil_o, g_final):
    batch, seq, d_model = x_prompt.shape
    bd, t_new, _ = x_sample.shape
    depth = g_layers.shape[0]
    n_a = w_mla_in.shape[0]
    q_lora = g_mla_q.shape[1]
    kv_lora = g_mla_kv.shape[1]
    past_len = page_table.shape[1] * cache_mla.shape[2]
    buffers = [state_dil_w128, state_dil_w512, state_dil_w2048]

    w_ffn_in_b = w_ffn_in.astype(BF16)
    w_ffn_out_b = w_ffn_out.astype(BF16)
    mla_w = [_prep_mla_weights(w_mla_in[a], w_mla_qb[a], w_mla_ukv[a], q_lora, kv_lora) for a in range(n_a)]
    w_mla_o_b = w_mla_o.astype(BF16)
    w_shared_b = _group_major_kv_cols(w_shared_kv).astype(BF16)
    w_dil_q_b = w_dil_q.astype(BF16)
    w_dil_o_b = w_dil_o.astype(BF16)
    kv_rope_blocks = []
    for _ in range(N_GROUPS):
        kv_rope_blocks += [True] * (GROUP_WIDTH // LANES) + [False] * (GROUP_WIDTH // LANES)
    q_rope_blocks = [True] * (DIL_WIDTH // LANES)

    pos_p = jnp.arange(seq, dtype=F32)
    pos_s = jnp.tile((past_len + jnp.arange(t_new)).astype(F32), bd)

    def trunk(x, pos, mix_a, mix_b):
        mla_tabs = _mla_tables(pos)
        dil_q_tabs = _dil_tables(pos, DIL_SCALE)
        dil_k_tabs = _dil_tables(pos, 1.0)
        rows_a = []
        kv_bf = kv_f32 = None
        for l in range(depth):
            x = _ffn(x, g_layers[l, 0], w_ffn_in_b[l, 0], w_ffn_out_b[l, 0], g_final)
            if l < n_a:
                x, rows = mix_a(x, l, mla_tabs)
                rows_a.append(rows)
            else:
                q = _norm_rope(x, g_layers[l, 1], w_dil_q_b[l - n_a], dil_q_tabs, q_rope_blocks, emit_f32=False)[0]
                outs, lses = mix_b(q, kv_bf, kv_f32)
                x = _dil_out(outs, lses, w_dil_o_b[l - n_a], x)
            x = _ffn(x, g_layers[l, 2], w_ffn_in_b[l, 1], w_ffn_out_b[l, 1], g_final,
                     final_norm=(l == depth - 1))
            if l == n_a - 1:
                kv_bf, kv_f32 = _norm_rope(x, g_shared_kv, w_shared_b, dil_k_tabs, kv_rope_blocks, emit_f32=True)
        return x, jnp.stack(rows_a, axis=0), kv_f32

    def mix_a_p(x, a, tabs):
        w_in_ext, w_qb_ext, w_ukv_ext, v_one, _, _ = mla_w[a]
        q, ckv, kr, k, v = _mla_proj(x, g_layers[a, 1], w_in_ext, g_mla_q[a], g_mla_kv[a], w_qb_ext, w_ukv_ext,
                                     tabs, v_one, with_kv=True)
        o = _mla_flash(q, k, v, batch, seq)
        x = _linear(o, w_mla_o_b[a], residual=x)
        rows = jnp.concatenate([ckv, kr[:, QK_NOPE:QK_NOPE + QK_ROPE]], axis=1)
        return x, rows

    def mix_b_p(q, kv_bf, kv_f32):
        res = [_dil_prompt_group(q, kv_bf, gi, dil, batch, seq) for gi, (_, dil) in enumerate(DIL_GROUPS)]
        return [r[0] for r in res], [r[1] for r in res]

    xp = x_prompt.reshape(batch * seq, d_model)
    y_p, rows_p, kv_p = trunk(xp, pos_p, mix_a_p, mix_b_p)
    y_prompt = y_p.reshape(batch, seq, d_model)
    mla_rows_prompt = rows_p.reshape(n_a, batch, seq, kv_lora + QK_ROPE)
    kv_p = kv_p.reshape(batch, seq, N_GROUPS, 2, HEADS_PER_GROUP, HEAD_DIM)
    dil_p = [kv_p[:, seq - min(win, seq):, gi] for gi, (win, _) in enumerate(DIL_GROUPS)]

    page = cache_mla.shape[2]

    def mix_a_s(x, a, tabs):
        w_in_ext, w_qb_ext, w_ukv_ext, v_one, w_uk_bd, w_uv_bd = mla_w[a]
        q, ckv, kr = _mla_proj(x, g_layers[a, 1], w_in_ext, g_mla_q[a], g_mla_kv[a], w_qb_ext, w_ukv_ext,
                               tabs, v_one, with_kv=False)
        rows = jnp.concatenate([ckv, kr[:, QK_NOPE:QK_NOPE + QK_ROPE]], axis=1)
        q_lat = _linear(q, w_uk_bd, out_dtype=BF16).reshape(bd, t_new * MLA_HEADS, kv_lora)
        q_rope = q.reshape(bd, t_new * MLA_HEADS, HEAD_BLOCK)[:, :, QK_NOPE:QK_NOPE + QK_ROPE]
        new_pad = jnp.pad(rows.reshape(bd, t_new, kv_lora + QK_ROPE), ((0, 0), (0, page - t_new), (0, 0)))
        o_lat = _mla_decode(q_lat, q_rope, new_pad, cache_mla[a], page_table)
        o = _linear(o_lat.reshape(bd * t_new, MLA_HEADS * kv_lora), w_uv_bd, out_dtype=BF16)
        x = _linear(o, w_mla_o_b[a], residual=x)
        return x, rows

    new_rows_pad = 8

    def mix_b_s(q, kv_bf, kv_f32):
        kv_new = kv_f32.reshape(bd, t_new, N_GROUPS, 2 * GROUP_WIDTH)
        kv_new = jnp.pad(kv_new, ((0, 0), (0, new_rows_pad - t_new), (0, 0), (0, 0)))
        qg = q.astype(F32).reshape(bd, t_new, N_GROUPS, GROUP_WIDTH)
        outs, lses = [], []
        for gi, (win, dil) in enumerate(DIL_GROUPS):
            buf = buffers[gi].reshape(bd, buffers[gi].shape[1], 2 * GROUP_WIDTH)
            o, lse = _dil_sample_group(qg[:, :, gi], buf, kv_new[:, :, gi], dil)
            outs.append(o)
            lses.append(lse)
        return outs, lses

    xs = x_sample.reshape(bd * t_new, d_model)
    y_s, rows_s, kv_s = trunk(xs, pos_s, mix_a_s, mix_b_s)
    y_sample = y_s.reshape(bd, t_new, d_model)
    mla_rows_sample = rows_s.reshape(n_a, bd, t_new, kv_lora + QK_ROPE)
    kv_s = kv_s.reshape(bd, t_new, N_GROUPS, 2, HEADS_PER_GROUP, HEAD_DIM)
    dil_s = []
    for gi, (win, _) in enumerate(DIL_GROUPS):
        buf = buffers[gi]
        length = buf.shape[1]
        keep = min(win, length + t_new)
        dil_s.append(jnp.concatenate([buf[:, length + t_new - keep:], kv_s[:, :, gi].astype(buf.dtype)], axis=1))

    return (y_prompt, y_sample, mla_rows_prompt, mla_rows_sample,
            dil_p[0], dil_p[1], dil_p[2], dil_s[0], dil_s[1], dil_s[2])
```

```python
import functools
import math

import numpy as np
import jax
import jax.numpy as jnp
from jax import lax
from jax.experimental import pallas as pl
from jax.experimental.pallas import tpu as pltpu

F32 = jnp.float32
BF16 = jnp.bfloat16

MLA_HEADS = 16
QK_NOPE = 64
QK_ROPE = 32
V_DIM = 64
MLA_THETA = 10000.0
MLA_SCALE = (QK_NOPE + QK_ROPE) ** -0.5
DIL_GROUPS = ((128, 1), (512, 4), (2048, 16))
N_GROUPS = len(DIL_GROUPS)
HEADS_PER_GROUP = 8
HEAD_DIM = 64
GROUP_WIDTH = HEADS_PER_GROUP * HEAD_DIM
DIL_WIDTH = N_GROUPS * GROUP_WIDTH
ROT_DIM = HEAD_DIM // 4
ROPE_THETA = 500000.0
DIL_SCALE = HEAD_DIM ** -0.5
DIL_N = 128
EPS = 1e-6
NEG = -1e30
LOG2E = math.log2(math.e)

LANES = 128
SUBLANES = 8
HEAD_BLOCK = 128
GROUP_LANE_BLOCKS = GROUP_WIDTH // LANES
VMEM_LIMIT_BYTES = 56 * 1024 * 1024
NT_DIMS = (((1,), (1,)), ((), ()))


def _cparams(*sem):
    return pltpu.CompilerParams(dimension_semantics=sem, vmem_limit_bytes=VMEM_LIMIT_BYTES)


def _row_tile(m, pref):
    t = min(m, pref)
    assert m % t == 0, (m, t)
    return t


def _rms(x, g):
    return x * lax.rsqrt(jnp.mean(x * x, axis=-1, keepdims=True) + EPS) * g


def _ffn_kernel(x_ref, g_ref, wg_ref, wu_ref, wo_ref, gf_ref, o_ref, h_sc, *, n_chunks, final_norm):
    c = pl.program_id(1)

    @pl.when(c == 0)
    def _():
        h_sc[...] = _rms(x_ref[...], g_ref[...]).astype(BF16)

    h = h_sc[...]
    gate = jnp.dot(h, wg_ref[...], preferred_element_type=F32)
    up = jnp.dot(h, wu_ref[...], preferred_element_type=F32)
    act = (gate / (1.0 + jnp.exp(-gate)) * up).astype(BF16)
    part = jnp.dot(act, wo_ref[...], preferred_element_type=F32)

    @pl.when(c == 0)
    def _():
        o_ref[...] = part

    @pl.when(c > 0)
    def _():
        o_ref[...] += part

    @pl.when(c == n_chunks - 1)
    def _():
        y = x_ref[...] + 0.5 * o_ref[...]
        if final_norm:
            y = _rms(y, gf_ref[...])
        o_ref[...] = y


def _ffn(x, g, w_in, w_out, layer, half, g_final, *, final_norm=False):
    m, d = x.shape
    d_ff = w_out.shape[2]
    tm = _row_tile(m, 512)
    n_chunks = 2 if d_ff % (2 * LANES) == 0 else 1
    tf = d_ff // n_chunks
    kern = functools.partial(_ffn_kernel, n_chunks=n_chunks, final_norm=final_norm)
    return pl.pallas_call(
        kern,
        grid=(m // tm, n_chunks),
        in_specs=[
            pl.BlockSpec((tm, d), lambda i, c: (i, 0)),
            pl.BlockSpec((1, d), lambda i, c: (0, 0)),
            pl.BlockSpec((None, None, d, tf), lambda i, c: (layer, half, 0, c)),
            pl.BlockSpec((None, None, d, tf), lambda i, c: (layer, half, 0, n_chunks + c)),
            pl.BlockSpec((None, None, tf, d), lambda i, c: (layer, half, c, 0)),
            pl.BlockSpec((1, d), lambda i, c: (0, 0)),
        ],
        out_specs=pl.BlockSpec((tm, d), lambda i, c: (i, 0)),
        out_shape=jax.ShapeDtypeStruct((m, d), F32),
        scratch_shapes=[pltpu.VMEM((tm, d), BF16)],
        compiler_params=_cparams("parallel", "arbitrary"),
        name="ffn",
    )(x, g.reshape(1, d), w_in, w_in, w_out, g_final.reshape(1, d))


def _linear_kernel(*refs, has_res):
    if has_res:
        a_ref, w_ref, r_ref, o_ref = refs
    else:
        a_ref, w_ref, o_ref = refs
    acc = jnp.dot(a_ref[...], w_ref[...], preferred_element_type=F32)
    if has_res:
        acc = acc + r_ref[...]
    o_ref[...] = acc.astype(o_ref.dtype)


def _linear(a, w, residual=None, out_dtype=F32):
    m, k = a.shape
    n = w.shape[1]
    tm = _row_tile(m, 512)
    tn = _row_tile(n, 1024)
    in_specs = [pl.BlockSpec((tm, k), lambda i, j: (i, 0)), pl.BlockSpec((k, tn), lambda i, j: (0, j))]
    args = [a, w]
    if residual is not None:
        in_specs.append(pl.BlockSpec((tm, tn), lambda i, j: (i, j)))
        args.append(residual)
    return pl.pallas_call(
        functools.partial(_linear_kernel, has_res=residual is not None),
        grid=(m // tm, n // tn),
        in_specs=in_specs,
        out_specs=pl.BlockSpec((tm, tn), lambda i, j: (i, j)),
        out_shape=jax.ShapeDtypeStruct((m, n), out_dtype),
        compiler_params=_cparams("parallel", "parallel"),
        name="linear",
    )(*args)


def _mla_proj_kernel(x_ref, g1_ref, win_ref, gq_ref, gkv_ref, wqb_ref, wukv_ref,
                     cq_ref, sq_ref, ck_ref, sk_ref, vone_ref,
                     q_ref, ckv_ref, kr_ref, *kv_refs, q_lora, kv_lora, with_kv):
    h = _rms(x_ref[...], g1_ref[...]).astype(BF16)
    comb = jnp.dot(h, win_ref[...], preferred_element_type=F32)
    cq = _rms(comb[:, :q_lora], gq_ref[...])
    ckv = _rms(comb[:, q_lora:q_lora + kv_lora], gkv_ref[...])
    blk = comb[:, q_lora + kv_lora:]
    kr = blk * ck_ref[...] + pltpu.roll(blk, HEAD_BLOCK - QK_ROPE, 1) * sk_ref[...]
    ckv_ref[...] = ckv
    kr_ref[...] = kr

    q = jnp.dot(cq.astype(BF16), wqb_ref[...], preferred_element_type=F32)
    cq_t = cq_ref[...]
    sq_t = sq_ref[...]
    for hd in range(MLA_HEADS):
        sl = slice(hd * HEAD_BLOCK, (hd + 1) * HEAD_BLOCK)
        qb = q[:, sl]
        q_ref[:, sl] = (qb * cq_t + pltpu.roll(qb, HEAD_BLOCK - QK_ROPE, 1) * sq_t).astype(BF16)

    if with_kv:
        k_ref, v_ref = kv_refs
        kv = jnp.dot(ckv.astype(BF16), wukv_ref[...], preferred_element_type=F32)
        width = MLA_HEADS * HEAD_BLOCK
        for hd in range(MLA_HEADS):
            sl = slice(hd * HEAD_BLOCK, (hd + 1) * HEAD_BLOCK)
            k_ref[:, sl] = (kv[:, sl] + kr).astype(BF16)
        v_ref[...] = (kv[:, width:] + vone_ref[...]).astype(BF16)


def _mla_proj(x, g1, w_in_ext, g_q, g_kv, w_qb_ext, w_ukv_ext, tabs, v_one, with_kv):
    m, d = x.shape
    q_lora = g_q.shape[0]
    kv_lora = g_kv.shape[0]
    tm = _row_tile(m, 512)
    width = MLA_HEADS * HEAD_BLOCK
    n_pos = tabs[0].shape[0] // tm
    full = lambda a: pl.BlockSpec(a.shape, lambda i: (0,) * a.ndim)
    tab_spec = pl.BlockSpec((tm, LANES), lambda i: (i % n_pos, 0))
    row = lambda n: pl.BlockSpec((tm, n), lambda i: (i, 0))
    out_shape = [jax.ShapeDtypeStruct((m, width), BF16), jax.ShapeDtypeStruct((m, kv_lora), F32),
                 jax.ShapeDtypeStruct((m, LANES), F32)]
    out_specs = [row(width), row(kv_lora), row(LANES)]
    if with_kv:
        out_shape += [jax.ShapeDtypeStruct((m, width), BF16), jax.ShapeDtypeStruct((m, width), BF16)]
        out_specs += [row(width), row(width)]
    g1 = g1.reshape(1, d)
    g_q = g_q.reshape(1, q_lora)
    g_kv = g_kv.reshape(1, kv_lora)
    return pl.pallas_call(
        functools.partial(_mla_proj_kernel, q_lora=q_lora, kv_lora=kv_lora, with_kv=with_kv),
        grid=(m // tm,),
        in_specs=[row(d), full(g1), full(w_in_ext), full(g_q), full(g_kv), full(w_qb_ext), full(w_ukv_ext),
                  tab_spec, tab_spec, tab_spec, tab_spec, full(v_one)],
        out_specs=out_specs,
        out_shape=out_shape,
        compiler_params=_cparams("parallel"),
        name="mla_proj",
    )(x, g1, w_in_ext, g_q, g_kv, w_qb_ext, w_ukv_ext, *tabs, v_one)


def _mla_flash_kernel(q_ref, k_ref, v_ref, o_ref, m_sc, acc_sc, s_sc, *, tq):
    qi = pl.program_id(2)
    tk = tq
    heads = (slice(0, HEAD_BLOCK), slice(HEAD_BLOCK, 2 * HEAD_BLOCK))
    m_sc[...] = jnp.full(m_sc.shape, NEG, F32)
    acc_sc[...] = jnp.zeros(acc_sc.shape, F32)

    def scores(kc, hh):
        start = pl.multiple_of(kc * tk, tk)
        return lax.dot_general(q_ref[:, heads[hh]], k_ref[pl.ds(start, tk), heads[hh]], NT_DIMS,
                               preferred_element_type=F32)

    def softmax_pv(kc, hh, s, masked):
        start = pl.multiple_of(kc * tk, tk)
        if masked:
            row = lax.broadcasted_iota(jnp.int32, (tq, tk), 0)
            col = lax.broadcasted_iota(jnp.int32, (tq, tk), 1)
            s = jnp.where(col <= row, s, NEG)
        m_prev = m_sc[hh]
        m_new = jnp.maximum(m_prev, jnp.max(s, axis=-1, keepdims=True))
        alpha = jnp.exp2(m_prev - m_new)
        p = jnp.exp2(s - pltpu.repeat(m_new, tk // LANES, axis=1))
        v = v_ref[pl.ds(start, tk), heads[hh]]
        acc_sc[hh] = acc_sc[hh] * alpha + jnp.dot(p.astype(BF16), v, preferred_element_type=F32)
        m_sc[hh] = m_new

    def chunk(kc, masked, prefetch):
        s_odd = scores(kc, 1)
        softmax_pv(kc, 0, s_sc[...], masked)
        softmax_pv(kc, 1, s_odd, masked)
        if prefetch:
            s_sc[...] = scores(kc + 1, 0)

    s_sc[...] = scores(0, 0)

    def body(kc, carry):
        chunk(kc, False, True)
        return carry

    lax.fori_loop(0, qi, body, 0)
    chunk(qi, True, False)
    lane = lax.broadcasted_iota(jnp.int32, (tq, HEAD_BLOCK), 1)
    acc0 = acc_sc[0]
    acc1 = acc_sc[1]
    out0 = acc0 / acc0[:, V_DIM:V_DIM + 1]
    out1 = acc1 / acc1[:, 0:1]
    o_ref[...] = jnp.where(lane < V_DIM, out0, out1).astype(o_ref.dtype)


def _mla_flash(q, k, v, batch, seq):
    width = MLA_HEADS * HEAD_BLOCK
    q = q.reshape(batch, seq, width)
    k = k.reshape(batch, seq, width)
    v = v.reshape(batch, seq, width)
    tq = _row_tile(seq, 512)
    pair = 2 * HEAD_BLOCK
    out = pl.pallas_call(
        functools.partial(_mla_flash_kernel, tq=tq),
        grid=(batch, MLA_HEADS // 2, seq // tq),
        in_specs=[
            pl.BlockSpec((None, tq, pair), lambda b, h, i: (b, i, h)),
            pl.BlockSpec((None, seq, pair), lambda b, h, i: (b, 0, h)),
            pl.BlockSpec((None, seq, pair), lambda b, h, i: (b, 0, h)),
        ],
        out_specs=pl.BlockSpec((None, tq, 2 * V_DIM), lambda b, h, i: (b, i, h)),
        out_shape=jax.ShapeDtypeStruct((batch, seq, MLA_HEADS * V_DIM), BF16),
        scratch_shapes=[pltpu.VMEM((2, tq, HEAD_BLOCK), F32), pltpu.VMEM((2, tq, HEAD_BLOCK), F32),
                        pltpu.VMEM((tq, tq), F32)],
        compiler_params=_cparams("parallel", "parallel", "arbitrary"),
        name="mla_flash",
    )(q, k, v)
    return out.reshape(batch * seq, MLA_HEADS * V_DIM)


def _mla_decode_kernel(pt_ref, ql_ref, qr_ref, newt_ref, *rest, n_pages_step, n_steps, page, kv_lora, t_new):
    page_refs = rest[:n_pages_step]
    o_ref = rest[n_pages_step]
    lat_sc, kr_sc, m_sc, l_sc, acc_sc = rest[n_pages_step + 1:]
    c = pl.program_id(1)
    rows = ql_ref.shape[0]

    @pl.when(c == 0)
    def _():
        m_sc[...] = jnp.full(m_sc.shape, NEG, F32)
        l_sc[...] = jnp.zeros(l_sc.shape, F32)
        acc_sc[...] = jnp.zeros(acc_sc.shape, F32)

    for i, ref in enumerate(list(page_refs) + [newt_ref]):
        lat_sc[:, i * page:(i + 1) * page] = ref[:kv_lora, :].astype(BF16)
        kr_sc[:, i * page:(i + 1) * page] = ref[kv_lora:, :].astype(BF16)

    s = (jnp.dot(ql_ref[...], lat_sc[...], preferred_element_type=F32)
         + jnp.dot(qr_ref[...], kr_sc[...], preferred_element_type=F32))
    n_keys = (n_pages_step + 1) * page
    col = lax.broadcasted_iota(jnp.int32, (rows, n_keys), 1)
    t_row = lax.shift_right_logical(lax.broadcasted_iota(jnp.int32, (rows, n_keys), 0),
                                    int(math.log2(MLA_HEADS)))
    j_new = col - n_pages_step * page
    n_new = jnp.where(c == n_steps - 1, t_new, 0)
    valid = (j_new < 0) | ((j_new <= t_row) & (j_new < n_new))
    s = jnp.where(valid, s, NEG)

    m_prev = m_sc[...]
    m_new = jnp.maximum(m_prev, jnp.max(s, axis=-1, keepdims=True))
    alpha = jnp.exp(m_prev - m_new)
    p = jnp.exp(s - m_new[:, :1])
    l_sc[...] = l_sc[...] * alpha + jnp.sum(p, axis=-1, keepdims=True)
    acc_sc[...] = acc_sc[...] * alpha[:, :1] + lax.dot_general(p.astype(BF16), lat_sc[...], NT_DIMS,
                                                                preferred_element_type=F32)
    m_sc[...] = m_new

    @pl.when(c == n_steps - 1)
    def _():
        o_ref[...] = (acc_sc[...] / l_sc[...][:, :1]).astype(o_ref.dtype)


def _mla_decode(q_lat, q_rope, new_t, cache_t, layer, page_table):
    bd, rows, kv_lora = q_lat.shape
    n_pages = page_table.shape[1]
    row_w, page = cache_t.shape[2], cache_t.shape[3]
    n_pages_step = min(n_pages, 32)
    assert n_pages % n_pages_step == 0
    n_steps = n_pages // n_pages_step
    t_new = rows // MLA_HEADS
    n_keys = (n_pages_step + 1) * page

    def page_spec(i):
        return pl.BlockSpec((None, None, row_w, page),
                            lambda b, c, pt: (layer, pt[b, c * n_pages_step + i], 0, 0))

    grid_spec = pltpu.PrefetchScalarGridSpec(
        num_scalar_prefetch=1,
        grid=(bd, n_steps),
        in_specs=[
            pl.BlockSpec((None, rows, kv_lora), lambda b, c, pt: (b, 0, 0)),
            pl.BlockSpec((None, rows, QK_ROPE), lambda b, c, pt: (b, 0, 0)),
            pl.BlockSpec((None, row_w, page), lambda b, c, pt: (b, 0, 0)),
        ] + [page_spec(i) for i in range(n_pages_step)],
        out_specs=pl.BlockSpec((None, rows, kv_lora), lambda b, c, pt: (b, 0, 0)),
        scratch_shapes=[
            pltpu.VMEM((kv_lora, n_keys), BF16),
            pltpu.VMEM((QK_ROPE, n_keys), BF16),
            pltpu.VMEM((rows, LANES), F32),
            pltpu.VMEM((rows, LANES), F32),
            pltpu.VMEM((rows, kv_lora), F32),
        ],
    )
    kern = functools.partial(_mla_decode_kernel, n_pages_step=n_pages_step, n_steps=n_steps, page=page,
                             kv_lora=kv_lora, t_new=t_new)
    return pl.pallas_call(
        kern,
        grid_spec=grid_spec,
        out_shape=jax.ShapeDtypeStruct((bd, rows, kv_lora), BF16),
        compiler_params=_cparams("parallel", "arbitrary"),
        name="mla_decode",
    )(page_table, q_lat, q_rope, new_t, *([cache_t] * n_pages_step))


def _norm_rope_kernel(x_ref, g_ref, w_ref, c_ref, s1_ref, s2_ref, *refs, rope_blocks, group_dils, group_blocks,
                      n_f32_out):
    n_groups = 0 if group_dils is None else len(group_dils)
    n_out = (n_groups if group_dils is not None else 0) + n_f32_out
    out_refs = refs[:n_out]
    sc = refs[n_out] if group_dils is not None else None
    tm = x_ref.shape[0]
    h = _rms(x_ref[...], g_ref[...]).astype(BF16)
    y = jnp.dot(h, w_ref[...], preferred_element_type=F32)
    ct, s1, s2 = c_ref[...], s1_ref[...], s2_ref[...]
    half = ROT_DIM // 2
    for j, roped in enumerate(rope_blocks):
        sl = slice(j * LANES, (j + 1) * LANES)
        blk = y[:, sl]
        if roped:
            blk = blk * ct + pltpu.roll(blk, LANES - half, 1) * s1 + pltpu.roll(blk, half, 1) * s2
        if n_f32_out:
            out_refs[n_out - 1][:, sl] = blk
        if group_dils is not None:
            gi, jj = divmod(j, group_blocks)
            dil = group_dils[gi]
            width = group_blocks * LANES
            if dil == 1:
                out_refs[gi][:, jj * LANES:(jj + 1) * LANES] = blk.astype(BF16)
            else:
                sc[...] = blk
                for r in range(dil):
                    lo = r * width + jj * LANES
                    out_refs[gi][:, lo:lo + LANES] = sc[pl.ds(r, tm // dil, stride=dil), :].astype(BF16)


def _norm_rope(x, g, w, tabs, rope_blocks, *, group_dils=None, group_blocks=None, emit_f32=False):
    m, d = x.shape
    n = w.shape[1]
    tm = _row_tile(m, 512)
    n_pos = tabs[0].shape[0] // tm
    tab_spec = pl.BlockSpec((tm, LANES), lambda i: (i % n_pos, 0))
    out_shape, out_specs, scratch = [], [], []
    if group_dils is not None:
        width = group_blocks * LANES
        for dil in group_dils:
            assert tm % (dil * 2 * SUBLANES) == 0
            out_shape.append(jax.ShapeDtypeStruct((m // dil, dil * width), BF16))
            out_specs.append(pl.BlockSpec((tm // dil, dil * width), lambda i: (i, 0)))
        scratch = [pltpu.VMEM((tm, LANES), F32)]
    else:
        emit_f32 = True
    if emit_f32:
        out_shape.append(jax.ShapeDtypeStruct((m, n), F32))
        out_specs.append(pl.BlockSpec((tm, n), lambda i: (i, 0)))
    return pl.pallas_call(
        functools.partial(_norm_rope_kernel, rope_blocks=tuple(rope_blocks), group_dils=group_dils,
                          group_blocks=group_blocks, n_f32_out=int(emit_f32)),
        grid=(m // tm,),
        in_specs=[pl.BlockSpec((tm, d), lambda i: (i, 0)), pl.BlockSpec((1, d), lambda i: (0, 0)),
                  pl.BlockSpec((d, n), lambda i: (0, 0)), tab_spec, tab_spec, tab_spec],
        out_specs=out_specs,
        out_shape=out_shape,
        scratch_shapes=scratch,
        compiler_params=_cparams("parallel"),
        name="norm_rope",
    )(x, g.reshape(1, d), w, *tabs)


def _dil_prompt_kernel(q_ref, kp_ref, kc_ref, vp_ref, vc_ref, o_ref, lse_ref, *, tq):
    cblk = pl.program_id(2)
    lane = lax.broadcasted_iota(jnp.int32, (tq, LANES), 1)
    i_idx = lax.broadcasted_iota(jnp.int32, (2 * tq, 2 * tq), 0) & (tq - 1)
    j_idx = lax.broadcasted_iota(jnp.int32, (2 * tq, 2 * tq), 1)
    prev_off = jnp.where(cblk > 0, 0, tq)
    valid = ((j_idx < tq) & (j_idx >= i_idx + prev_off)) | ((j_idx >= tq) & (j_idx - tq <= i_idx))
    for hp in range(HEADS_PER_GROUP // 2):
        sl = slice(hp * LANES, (hp + 1) * LANES)
        q2 = q_ref[:, sl]
        zero = jnp.zeros_like(q2)
        qs = jnp.concatenate([jnp.where(lane < HEAD_DIM, q2, zero), jnp.where(lane >= HEAD_DIM, q2, zero)], axis=0)
        kcat = jnp.concatenate([kp_ref[:, sl], kc_ref[:, sl]], axis=0)
        vcat = jnp.concatenate([vp_ref[:, sl], vc_ref[:, sl]], axis=0)
        s = jnp.where(valid, lax.dot_general(qs, kcat, NT_DIMS, preferred_element_type=F32), NEG)
        m = jnp.max(s, axis=-1, keepdims=True)
        p = jnp.exp(s - m)
        den = jnp.sum(p, axis=-1, keepdims=True)
        o2 = jnp.dot(p.astype(BF16), vcat, preferred_element_type=F32) / den
        lse2 = jnp.broadcast_to(m + jnp.log(den), (2 * tq, LANES))
        o_ref[:, sl] = jnp.where(lane < HEAD_DIM, o2[:tq], o2[tq:])
        lse_ref[:, sl] = jnp.where(lane < HEAD_DIM, lse2[:tq], lse2[tq:])


def _dil_prompt_group(q_g, kv_g, dil, batch, seq):
    assert seq % (dil * DIL_N) == 0
    rows = seq // dil
    tq = DIL_N
    nblk = rows // tq
    qv = q_g.reshape(batch, rows, dil * GROUP_WIDTH)
    kvv = kv_g.reshape(batch, rows, dil * 2 * GROUP_WIDTH)
    blk = (None, tq, GROUP_WIDTH)
    cur = lambda off: pl.BlockSpec(blk, lambda b, r, c: (b, c, 2 * r + off))
    prev = lambda off: pl.BlockSpec(blk, lambda b, r, c: (b, jnp.maximum(c - 1, 0), 2 * r + off))
    rspec = pl.BlockSpec(blk, lambda b, r, c: (b, c, r))
    o, lse = pl.pallas_call(
        functools.partial(_dil_prompt_kernel, tq=tq),
        grid=(batch, dil, nblk),
        in_specs=[rspec, prev(0), cur(0), prev(1), cur(1)],
        out_specs=[rspec, rspec],
        out_shape=[jax.ShapeDtypeStruct((batch, rows, dil * GROUP_WIDTH), F32)] * 2,
        compiler_params=_cparams("parallel", "parallel", "arbitrary"),
        name="dil_prompt",
    )(qv, kvv, kvv, kvv, kvv)
    return o.reshape(batch * rows, dil * GROUP_WIDTH), lse.reshape(batch * rows, dil * GROUP_WIDTH)


def _dil_sample_kernel(q_ref, buf_ref, new_ref, *rest, bb, dil, t_new, emit_state):
    if emit_state:
        newt_ref, o_ref, lse_ref, state_ref = rest
    else:
        o_ref, lse_ref = rest
    w = buf_ref.shape[2]
    rows = t_new * HEADS_PER_GROUP
    log_h = int(math.log2(HEADS_PER_GROUP))
    lane = lax.broadcasted_iota(jnp.int32, (rows, GROUP_WIDTH), 1)
    row = lax.broadcasted_iota(jnp.int32, (rows, GROUP_WIDTH), 0)
    diag = lax.shift_right_logical(lane, int(math.log2(HEAD_DIM))) == (row & (HEADS_PER_GROUP - 1))
    w_idx = lax.broadcasted_iota(jnp.int32, (rows, w), 1)
    t_of_row = lax.shift_right_logical(lax.broadcasted_iota(jnp.int32, (rows, w), 0), log_h)
    valid = (w_idx >= t_of_row) if dil == 1 else ((w_idx & (dil - 1)) == t_of_row)
    n_new = new_ref.shape[1]
    j_new = lax.broadcasted_iota(jnp.int32, (rows, n_new), 1)
    t_new_row = lax.shift_right_logical(lax.broadcasted_iota(jnp.int32, (rows, n_new), 0), log_h)
    valid_new = (j_new <= t_new_row) if dil == 1 else (j_new == t_new_row)
    shift_lane = lax.broadcasted_iota(jnp.int32, (2 * GROUP_WIDTH, LANES), 1)
    for b in range(bb):
        q = q_ref[b]
        qrep = jnp.concatenate(
            [jnp.broadcast_to(q[t:t + 1, :], (HEADS_PER_GROUP, GROUP_WIDTH)) for t in range(t_new)], axis=0)
        qbd = jnp.where(diag, qrep, 0.0).astype(BF16)
        kt = buf_ref[b, :GROUP_WIDTH, :].astype(BF16)
        vt = buf_ref[b, GROUP_WIDTH:, :].astype(BF16)
        k_new = new_ref[b, :, :GROUP_WIDTH].astype(BF16)
        v_new = new_ref[b, :, GROUP_WIDTH:].astype(BF16)
        s = jnp.where(valid, jnp.dot(qbd, kt, preferred_element_type=F32), NEG)
        s_new = jnp.where(valid_new, lax.dot_general(qbd, k_new, NT_DIMS, preferred_element_type=F32), NEG)
        m = jnp.maximum(jnp.max(s, axis=-1, keepdims=True), jnp.max(s_new, axis=-1, keepdims=True))
        p = jnp.exp(s - m)
        p_new = jnp.exp(s_new - m)
        den = jnp.sum(p, axis=-1, keepdims=True) + jnp.sum(p_new, axis=-1, keepdims=True)
        o = (lax.dot_general(p.astype(BF16), vt, NT_DIMS, preferred_element_type=F32)
             + jnp.dot(p_new.astype(BF16), v_new, preferred_element_type=F32)) / den
        lse = jnp.broadcast_to(m + jnp.log(den), (rows, GROUP_WIDTH))
        o_ref[b] = jnp.sum(jnp.where(diag, o, 0.0).reshape(t_new, HEADS_PER_GROUP, GROUP_WIDTH), axis=1)
        lse_ref[b] = jnp.sum(jnp.where(diag, lse, 0.0).reshape(t_new, HEADS_PER_GROUP, GROUP_WIDTH), axis=1)
        if emit_state:
            n_col = w // LANES
            nxt = pltpu.roll(buf_ref[b, :, 0:LANES], LANES - t_new, 1)
            for jc in range(n_col):
                cur = nxt
                if jc + 1 < n_col:
                    nxt = pltpu.roll(buf_ref[b, :, (jc + 1) * LANES:(jc + 2) * LANES], LANES - t_new, 1)
                    cur = jnp.where(shift_lane < LANES - t_new, cur, nxt)
                state_ref[b, :, jc * LANES:(jc + 1) * LANES] = cur
            state_ref[b, :, w - t_new:w] = newt_ref[b][:, :t_new]


def _dil_sample_group(q_g, buf_t, new_g, new_t, dil, emit_state):
    bd, t_new, _ = q_g.shape
    w = buf_t.shape[2]
    assert w == dil * DIL_N, "state buffer must hold exactly one window"
    assert dil == 1 or dil >= t_new
    kv_w = 2 * GROUP_WIDTH
    bb = _row_tile(bd, max(1, 1024 // w))
    spec3 = lambda a: pl.BlockSpec((bb,) + a.shape[1:], lambda i: (i, 0, 0))
    in_specs = [spec3(q_g), spec3(buf_t), spec3(new_g)]
    args = [q_g, buf_t, new_g]
    out_spec = pl.BlockSpec((bb, t_new, GROUP_WIDTH), lambda i: (i, 0, 0))
    out_specs = [out_spec, out_spec]
    out_shape = [jax.ShapeDtypeStruct((bd, t_new, GROUP_WIDTH), F32)] * 2
    if emit_state:
        in_specs.append(spec3(new_t))
        args.append(new_t)
        out_specs.append(spec3(buf_t))
        out_shape.append(jax.ShapeDtypeStruct(buf_t.shape, F32))
    res = pl.pallas_call(
        functools.partial(_dil_sample_kernel, bb=bb, dil=dil, t_new=t_new, emit_state=emit_state),
        grid=(bd // bb,),
        in_specs=in_specs,
        out_specs=out_specs,
        out_shape=out_shape,
        compiler_params=_cparams("parallel"),
        name="dil_sample",
    )(*args)
    o, lse = res[0].reshape(bd * t_new, GROUP_WIDTH), res[1].reshape(bd * t_new, GROUP_WIDTH)
    return o, lse, (res[2] if emit_state else None)


def _dil_out_kernel(*refs, group_dils):
    n_g = N_GROUPS
    o_refs, l_refs = refs[:n_g], refs[n_g:2 * n_g]
    w_ref, x_ref, y_ref = refs[2 * n_g:2 * n_g + 3]
    scratch = refs[2 * n_g + 3:]
    tm = x_ref.shape[0]
    cols = []
    for jj in range(GROUP_LANE_BLOCKS):
        sl = slice(jj * LANES, (jj + 1) * LANES)
        os_, ls_ = [], []
        for gi in range(n_g):
            dil = 1 if group_dils is None else group_dils[gi]
            if dil == 1:
                os_.append(o_refs[gi][:, sl])
                ls_.append(l_refs[gi][:, sl])
            else:
                for src, dst in ((o_refs[gi], scratch[0]), (l_refs[gi], scratch[1])):
                    for r in range(dil):
                        lo = r * GROUP_WIDTH + jj * LANES
                        dst[pl.ds(r, tm // dil, stride=dil), :] = src[:, lo:lo + LANES]
                os_.append(scratch[0][...])
                ls_.append(scratch[1][...])
        m = jnp.maximum(jnp.maximum(ls_[0], ls_[1]), ls_[2])
        e = [jnp.exp(l - m) for l in ls_]
        den = e[0] + e[1] + e[2]
        cols.append(((e[0] / den) * os_[0] + (e[1] / den) * os_[1] + (e[2] / den) * os_[2]).astype(BF16))
    o = jnp.concatenate(cols, axis=1)
    y_ref[...] = x_ref[...] + jnp.dot(o, w_ref[...], preferred_element_type=F32)


def _dil_out(outs, lses, w_o, x, group_dils=None):
    m, d = x.shape
    tm = _row_tile(m, 512)
    specs = []
    for gi in range(N_GROUPS):
        dil = 1 if group_dils is None else group_dils[gi]
        specs.append(pl.BlockSpec((tm // dil, dil * GROUP_WIDTH), lambda i: (i, 0)))
    scratch = [] if group_dils is None else [pltpu.VMEM((tm, LANES), F32)] * 2
    return pl.pallas_call(
        functools.partial(_dil_out_kernel, group_dils=group_dils),
        grid=(m // tm,),
        in_specs=specs + specs + [pl.BlockSpec(w_o.shape, lambda i: (0, 0)), pl.BlockSpec((tm, d), lambda i: (i, 0))],
        out_specs=pl.BlockSpec((tm, d), lambda i: (i, 0)),
        out_shape=jax.ShapeDtypeStruct((m, d), F32),
        scratch_shapes=scratch,
        compiler_params=_cparams("parallel"),
        name="dil_out",
    )(*outs, *lses, w_o, x)


def _rope_cos_sin(pos, rot_dim, theta):
    half = rot_dim // 2
    inv = jnp.float32(theta) ** (-2.0 * jnp.arange(half, dtype=F32) / rot_dim)
    ang = pos[:, None] * inv[None, :]
    return jnp.cos(ang), jnp.sin(ang)


def _mla_tables(pos, q_scale):
    cos, sin = _rope_cos_sin(pos, QK_ROPE, MLA_THETA)
    n = pos.shape[0]
    z = lambda w: jnp.zeros((n, w), F32)
    cos2 = jnp.concatenate([cos, cos], axis=1)
    sin2 = jnp.concatenate([sin, sin], axis=1)
    rest = HEAD_BLOCK - QK_NOPE - QK_ROPE
    cq = jnp.concatenate([jnp.ones((n, QK_NOPE), F32), cos2, z(rest)], axis=1) * q_scale
    sq = jnp.concatenate([z(QK_NOPE), sin2, z(rest)], axis=1) * q_scale
    ck = jnp.concatenate([z(QK_NOPE), cos2, z(rest)], axis=1)
    sk = jnp.concatenate([z(QK_NOPE), sin2, z(rest)], axis=1)
    return cq, sq, ck, sk


def _dil_tables(pos, scale):
    cos, sin = _rope_cos_sin(pos, ROT_DIM, ROPE_THETA)
    n = pos.shape[0]
    half = ROT_DIM // 2
    z = lambda w: jnp.zeros((n, w), F32)
    keep = jnp.ones((n, HEAD_DIM - ROT_DIM), F32)
    c_head = jnp.concatenate([cos, cos, keep], axis=1)
    s1_head = jnp.concatenate([-sin, z(HEAD_DIM - half)], axis=1)
    s2_head = jnp.concatenate([z(half), sin, z(HEAD_DIM - ROT_DIM)], axis=1)
    two = lambda a: jnp.concatenate([a, a], axis=1) * scale
    return two(c_head), two(s1_head), two(s2_head)


def _rot_half_cols(w):
    half = w.shape[-1] // 2
    return jnp.concatenate([-w[..., half:], w[..., :half]], axis=-1)


def _prep_mla_weights(w_in, w_qb, w_ukv, q_lora, kv_lora):
    d = w_in.shape[0]
    w_kr = w_in[:, q_lora + kv_lora:]
    assert HEAD_BLOCK - QK_NOPE - QK_ROPE == QK_ROPE
    w_in_ext = jnp.concatenate(
        [w_in[:, :q_lora + kv_lora], jnp.zeros((d, QK_NOPE), F32), w_kr, _rot_half_cols(w_kr)], axis=1)
    wq = w_qb.reshape(q_lora, MLA_HEADS, QK_NOPE + QK_ROPE)
    wq_rope = wq[..., QK_NOPE:]
    w_qb_ext = jnp.concatenate([wq[..., :QK_NOPE], wq_rope, _rot_half_cols(wq_rope)], axis=-1)
    w_qb_ext = w_qb_ext.reshape(q_lora, MLA_HEADS * HEAD_BLOCK)
    wkv = w_ukv.reshape(kv_lora, MLA_HEADS, QK_NOPE + V_DIM)
    zpad = jnp.zeros((kv_lora, MLA_HEADS, HEAD_BLOCK - QK_NOPE), F32)
    wk = jnp.concatenate([wkv[..., :QK_NOPE], zpad], axis=-1).reshape(kv_lora, MLA_HEADS * HEAD_BLOCK)
    wv_even = jnp.concatenate([wkv[..., QK_NOPE:], zpad], axis=-1)
    wv_odd = jnp.concatenate([zpad, wkv[..., QK_NOPE:]], axis=-1)
    odd = (jnp.arange(MLA_HEADS) % 2 == 1)[None, :, None]
    wv = jnp.where(odd, wv_odd, wv_even).reshape(kv_lora, MLA_HEADS * HEAD_BLOCK)
    w_ukv_ext = jnp.concatenate([wk, wv], axis=1)
    one = np.zeros((1, MLA_HEADS, HEAD_BLOCK), np.float32)
    one[0, 0::2, V_DIM] = 1.0
    one[0, 1::2, 0] = 1.0
    v_one = jnp.asarray(one.reshape(1, MLA_HEADS * HEAD_BLOCK))
    eye = jnp.eye(MLA_HEADS, dtype=F32)
    w_uk = wkv[..., :QK_NOPE]
    w_uk_rows = jnp.concatenate([jnp.transpose(w_uk, (1, 2, 0)),
                                 jnp.zeros((MLA_HEADS, HEAD_BLOCK - QK_NOPE, kv_lora), F32)], axis=1)
    w_uk_bd = jnp.einsum('hnc,hg->hngc', w_uk_rows, eye).reshape(MLA_HEADS * HEAD_BLOCK, MLA_HEADS * kv_lora)
    w_uv = jnp.transpose(wkv[..., QK_NOPE:], (1, 0, 2))
    w_uv_bd = jnp.einsum('hcv,hg->hcgv', w_uv, eye).reshape(MLA_HEADS * kv_lora, MLA_HEADS * V_DIM)
    bf = lambda a: a.astype(BF16)
    return bf(w_in_ext), bf(w_qb_ext), bf(w_ukv_ext), v_one, bf(w_uk_bd), bf(w_uv_bd)


def _group_major_kv_cols(w_shared):
    d = w_shared.shape[0]
    w = w_shared.reshape(d, 2, N_GROUPS, GROUP_WIDTH)
    return jnp.transpose(w, (0, 2, 1, 3)).reshape(d, 2 * DIL_WIDTH)


def kernel(x_prompt, x_sample, cache_mla, page_table, state_dil_w128, state_dil_w512, state_dil_w2048,
           g_layers, w_ffn_in, w_ffn_out, w_mla_in, g_mla_q, g_mla_kv, w_mla_qb, w_mla_ukv, w_mla_o,
           g_shared_kv, w_shared_kv, w_dil_q, w_dil_o, g_final):
    batch, seq, d_model = x_prompt.shape
    bd, t_new, _ = x_sample.shape
    depth = g_layers.shape[0]
    n_a = w_mla_in.shape[0]
    q_lora = g_mla_q.shape[1]
    kv_lora = g_mla_kv.shape[1]
    page = cache_mla.shape[2]
    past_len = page_table.shape[1] * page
    buffers = [state_dil_w128, state_dil_w512, state_dil_w2048]
    dils = tuple(dil for _, dil in DIL_GROUPS)
    kv_w = 2 * GROUP_WIDTH

    w_ffn_in_b = w_ffn_in.astype(BF16)
    w_ffn_out_b = w_ffn_out.astype(BF16)
    mla_w = [_prep_mla_weights(w_mla_in[a], w_mla_qb[a], w_mla_ukv[a], q_lora, kv_lora) for a in range(n_a)]
    w_mla_o_b = w_mla_o.astype(BF16)
    w_shared_b = _group_major_kv_cols(w_shared_kv).astype(BF16)
    w_dil_q_b = w_dil_q.astype(BF16)
    w_dil_o_b = w_dil_o.astype(BF16)
    kv_rope_blocks = []
    for _ in range(N_GROUPS):
        kv_rope_blocks += [True] * GROUP_LANE_BLOCKS + [False] * GROUP_LANE_BLOCKS
    q_rope_blocks = [True] * (DIL_WIDTH // LANES)

    pos_p = jnp.arange(seq, dtype=F32)
    pos_s = jnp.tile((past_len + jnp.arange(t_new)).astype(F32), bd)

    def ffn(x, l, half, final_norm=False):
        return _ffn(x, g_layers[l, 2 * half], w_ffn_in_b, w_ffn_out_b, l, half, g_final, final_norm=final_norm)

    mla_tabs_p = _mla_tables(pos_p, MLA_SCALE * LOG2E)
    dil_q_tabs_p = _dil_tables(pos_p, DIL_SCALE)
    dil_k_tabs_p = _dil_tables(pos_p, 1.0)
    x = x_prompt.reshape(batch * seq, d_model)
    rows_p = []
    kv_groups = kv_f32_p = None
    for l in range(depth):
        x = ffn(x, l, 0)
        if l < n_a:
            w_in_ext, w_qb_ext, w_ukv_ext, v_one, _, _ = mla_w[l]
            q, ckv, kr, k, v = _mla_proj(x, g_layers[l, 1], w_in_ext, g_mla_q[l], g_mla_kv[l], w_qb_ext, w_ukv_ext,
                                         mla_tabs_p, v_one, with_kv=True)
            o = _mla_flash(q, k, v, batch, seq)
            x = _linear(o, w_mla_o_b[l], residual=x)
            rows_p.append(jnp.concatenate([ckv, kr[:, QK_NOPE:QK_NOPE + QK_ROPE]], axis=1))
        else:
            q_groups = _norm_rope(x, g_layers[l, 1], w_dil_q_b[l - n_a], dil_q_tabs_p, q_rope_blocks,
                                  group_dils=dils, group_blocks=GROUP_LANE_BLOCKS)
            res = [_dil_prompt_group(q_groups[gi], kv_groups[gi], dil, batch, seq) for gi, dil in enumerate(dils)]
            x = _dil_out([r[0] for r in res], [r[1] for r in res], w_dil_o_b[l - n_a], x, group_dils=dils)
        x = ffn(x, l, 1, final_norm=(l == depth - 1))
        if l == n_a - 1:
            *kv_groups, kv_f32_p = _norm_rope(x, g_shared_kv, w_shared_b, dil_k_tabs_p, kv_rope_blocks,
                                              group_dils=dils, group_blocks=2 * GROUP_LANE_BLOCKS, emit_f32=True)
    y_prompt = x.reshape(batch, seq, d_model)
    mla_rows_prompt = jnp.stack(rows_p, axis=0).reshape(n_a, batch, seq, kv_lora + QK_ROPE)
    kv_p3 = kv_f32_p.reshape(batch, seq, N_GROUPS * kv_w)
    dil_p = []
    for gi, (win, _) in enumerate(DIL_GROUPS):
        keep = min(win, seq)
        tail = kv_p3[:, seq - keep:, gi * kv_w:(gi + 1) * kv_w]
        dil_p.append(tail.reshape(batch, keep, 2, HEADS_PER_GROUP, HEAD_DIM))

    mla_tabs_s = _mla_tables(pos_s, MLA_SCALE)
    dil_q_tabs_s = _dil_tables(pos_s, DIL_SCALE)
    dil_k_tabs_s = _dil_tables(pos_s, 1.0)
    cache_t = jnp.swapaxes(cache_mla, 2, 3)
    bufs_t = []
    for buf in buffers:
        length = buf.shape[1]
        assert min(length, length + t_new) == length
        bufs_t.append(jnp.transpose(buf, (0, 2, 3, 4, 1)).reshape(bd, kv_w, length))
    new_rows_pad = SUBLANES
    x = x_sample.reshape(bd * t_new, d_model)
    rows_s = []
    kv_new = kv_new_t = None
    states_t = [None] * N_GROUPS
    for l in range(depth):
        x = ffn(x, l, 0)
        if l < n_a:
            w_in_ext, w_qb_ext, w_ukv_ext, v_one, w_uk_bd, w_uv_bd = mla_w[l]
            q, ckv, kr = _mla_proj(x, g_layers[l, 1], w_in_ext, g_mla_q[l], g_mla_kv[l], w_qb_ext, w_ukv_ext,
                                   mla_tabs_s, v_one, with_kv=False)
            rows = jnp.concatenate([ckv, kr[:, QK_NOPE:QK_NOPE + QK_ROPE]], axis=1)
            rows_s.append(rows)
            q_lat = _linear(q, w_uk_bd, out_dtype=BF16).reshape(bd, t_new * MLA_HEADS, kv_lora)
            q_rope = q.reshape(bd, t_new * MLA_HEADS, HEAD_BLOCK)[:, :, QK_NOPE:QK_NOPE + QK_ROPE]
            new_pad = jnp.pad(rows.reshape(bd, t_new, kv_lora + QK_ROPE), ((0, 0), (0, page - t_new), (0, 0)))
            o_lat = _mla_decode(q_lat, q_rope, jnp.swapaxes(new_pad, 1, 2), cache_t, l, page_table)
            o = _linear(o_lat.reshape(bd * t_new, MLA_HEADS * kv_lora), w_uv_bd, out_dtype=BF16)
            x = _linear(o, w_mla_o_b[l], residual=x)
        else:
            q = _norm_rope(x, g_layers[l, 1], w_dil_q_b[l - n_a], dil_q_tabs_s, q_rope_blocks)[0]
            qg = q.reshape(bd, t_new, N_GROUPS, GROUP_WIDTH)
            outs, lses = [], []
            for gi, dil in enumerate(dils):
                emit = l == n_a
                o, lse, st = _dil_sample_group(qg[:, :, gi], bufs_t[gi], kv_new[:, :, gi], kv_new_t[:, gi], dil, emit)
                if emit:
                    states_t[gi] = st
                outs.append(o)
                lses.append(lse)
            x = _dil_out(outs, lses, w_dil_o_b[l - n_a], x)
        x = ffn(x, l, 1, final_norm=(l == depth - 1))
        if l == n_a - 1:
            kv_f32_s = _norm_rope(x, g_shared_kv, w_shared_b, dil_k_tabs_s, kv_rope_blocks)[0]
            kv_s = kv_f32_s.reshape(bd, t_new, N_GROUPS, kv_w)
            kv_new = jnp.pad(kv_s, ((0, 0), (0, new_rows_pad - t_new), (0, 0), (0, 0)))
            kv_new_t = jnp.transpose(kv_new, (0, 2, 3, 1))
    y_sample = x.reshape(bd, t_new, d_model)
    mla_rows_sample = jnp.stack(rows_s, axis=0).reshape(n_a, bd, t_new, kv_lora + QK_ROPE)
    dil_s = []
    for gi, buf in enumerate(buffers):
        length = buf.shape[1]
        st = states_t[gi].reshape(bd, 2, HEADS_PER_GROUP, HEAD_DIM, length)
        dil_s.append(jnp.transpose(st, (0, 4, 1, 2, 3)))

    return (y_prompt, y_sample, mla_rows_prompt, mla_rows_sample,
            dil_p[0], dil_p[1], dil_p[2], dil_s[0], dil_s[1], dil_s[2])
```

```python
import functools
import math

import numpy as np
import jax
import jax.numpy as jnp
from jax import lax
from jax.experimental import pallas as pl
from jax.experimental.pallas import tpu as pltpu

F32 = jnp.float32
BF16 = jnp.bfloat16

MLA_HEADS = 16
QK_NOPE = 64
QK_ROPE = 32
V_DIM = 64
MLA_THETA = 10000.0
MLA_SCALE = (QK_NOPE + QK_ROPE) ** -0.5
DIL_GROUPS = ((128, 1), (512, 4), (2048, 16))
N_GROUPS = len(DIL_GROUPS)
HEADS_PER_GROUP = 8
HEAD_DIM = 64
GROUP_WIDTH = HEADS_PER_GROUP * HEAD_DIM
DIL_WIDTH = N_GROUPS * GROUP_WIDTH
ROT_DIM = HEAD_DIM // 4
ROPE_THETA = 500000.0
DIL_SCALE = HEAD_DIM ** -0.5
DIL_N = 128
EPS = 1e-6
NEG = -1e30
LOG2E = math.log2(math.e)

LANES = 128
SUBLANES = 8
HEAD_BLOCK = 128
GROUP_LANE_BLOCKS = GROUP_WIDTH // LANES
MXU_DIM = 256
FFN_SUB_COLS = 2 * MXU_DIM
VMEM_LIMIT_BYTES = 56 * 1024 * 1024
NT_DIMS = (((1,), (1,)), ((), ()))


def _cparams(*sem):
    return pltpu.CompilerParams(dimension_semantics=sem, vmem_limit_bytes=VMEM_LIMIT_BYTES)


def _row_tile(m, pref):
    t = min(m, pref)
    assert m % t == 0, (m, t)
    return t


def _rms(x, g):
    return x * lax.rsqrt(jnp.mean(x * x, axis=-1, keepdims=True) + EPS) * g


def _ffn_kernel(x_ref, g_ref, wg_ref, wu_ref, wo_ref, gf_ref, o_ref, *, final_norm, sub):
    x = x_ref[...]
    h = _rms(x, g_ref[...]).astype(BF16)
    d_ff = wg_ref.shape[1]
    part = None
    for lo in range(0, d_ff, sub):
        hi = min(lo + sub, d_ff)
        gate = jnp.dot(h, wg_ref[:, lo:hi], preferred_element_type=F32)
        up = jnp.dot(h, wu_ref[:, lo:hi], preferred_element_type=F32)
        act = (gate / (1.0 + jnp.exp(-gate)) * up).astype(BF16)
        contrib = jnp.dot(act, wo_ref[lo:hi, :], preferred_element_type=F32)
        part = contrib if part is None else part + contrib
    y = x + 0.5 * part
    if final_norm:
        y = _rms(y, gf_ref[...])
    o_ref[...] = y


def _ffn(x, g, w_in, w_out, layer, half, g_final, *, final_norm=False):
    m, d = x.shape
    d_ff = w_out.shape[2]
    tm = _row_tile(m, 512)
    kern = functools.partial(_ffn_kernel, final_norm=final_norm, sub=min(d_ff, FFN_SUB_COLS))
    resident = dict(pipeline_mode=pl.Buffered(1))
    return pl.pallas_call(
        kern,
        grid=(m // tm,),
        in_specs=[
            pl.BlockSpec((tm, d), lambda i: (i, 0)),
            pl.BlockSpec((1, d), lambda i: (0, 0)),
            pl.BlockSpec((None, None, d, d_ff), lambda i: (layer, half, 0, 0), **resident),
            pl.BlockSpec((None, None, d, d_ff), lambda i: (layer, half, 0, 1), **resident),
            pl.BlockSpec((None, None, d_ff, d), lambda i: (layer, half, 0, 0), **resident),
            pl.BlockSpec((1, d), lambda i: (0, 0)),
        ],
        out_specs=pl.BlockSpec((tm, d), lambda i: (i, 0)),
        out_shape=jax.ShapeDtypeStruct((m, d), F32),
        compiler_params=_cparams("parallel"),
        name="ffn",
    )(x, g.reshape(1, d), w_in, w_in, w_out, g_final.reshape(1, d))


def _linear_kernel(*refs, has_res):
    if has_res:
        a_ref, w_ref, r_ref, o_ref = refs
    else:
        a_ref, w_ref, o_ref = refs
    acc = jnp.dot(a_ref[...], w_ref[...], preferred_element_type=F32)
    if has_res:
        acc = acc + r_ref[...]
    o_ref[...] = acc.astype(o_ref.dtype)


def _linear(a, w, residual=None, out_dtype=F32):
    m, k = a.shape
    n = w.shape[1]
    tm = _row_tile(m, 512)
    tn = _row_tile(n, 1024)
    in_specs = [pl.BlockSpec((tm, k), lambda i, j: (i, 0)), pl.BlockSpec((k, tn), lambda i, j: (0, j))]
    args = [a, w]
    if residual is not None:
        in_specs.append(pl.BlockSpec((tm, tn), lambda i, j: (i, j)))
        args.append(residual)
    return pl.pallas_call(
        functools.partial(_linear_kernel, has_res=residual is not None),
        grid=(m // tm, n // tn),
        in_specs=in_specs,
        out_specs=pl.BlockSpec((tm, tn), lambda i, j: (i, j)),
        out_shape=jax.ShapeDtypeStruct((m, n), out_dtype),
        compiler_params=_cparams("parallel", "parallel"),
        name="linear",
    )(*args)


def _mla_proj_kernel(x_ref, g1_ref, win_ref, gq_ref, gkv_ref, wqb_ref, wukv_ref,
                     cq_ref, sq_ref, ck_ref, sk_ref, vone_ref,
                     q_ref, ckv_ref, kr_ref, *kv_refs, q_lora, kv_lora, with_kv):
    h = _rms(x_ref[...], g1_ref[...]).astype(BF16)
    comb = jnp.dot(h, win_ref[...], preferred_element_type=F32)
    cq = _rms(comb[:, :q_lora], gq_ref[...])
    ckv = _rms(comb[:, q_lora:q_lora + kv_lora], gkv_ref[...])
    blk = comb[:, q_lora + kv_lora:]
    kr = blk * ck_ref[...] + pltpu.roll(blk, HEAD_BLOCK - QK_ROPE, 1) * sk_ref[...]
    ckv_ref[...] = ckv
    kr_ref[...] = kr

    q = jnp.dot(cq.astype(BF16), wqb_ref[...], preferred_element_type=F32)
    cq_t = cq_ref[...]
    sq_t = sq_ref[...]
    for hd in range(MLA_HEADS):
        sl = slice(hd * HEAD_BLOCK, (hd + 1) * HEAD_BLOCK)
        qb = q[:, sl]
        q_ref[:, sl] = (qb * cq_t + pltpu.roll(qb, HEAD_BLOCK - QK_ROPE, 1) * sq_t).astype(BF16)

    if with_kv:
        k_ref, v_ref = kv_refs
        kv = jnp.dot(ckv.astype(BF16), wukv_ref[...], preferred_element_type=F32)
        width = MLA_HEADS * HEAD_BLOCK
        for hd in range(MLA_HEADS):
            sl = slice(hd * HEAD_BLOCK, (hd + 1) * HEAD_BLOCK)
            k_ref[:, sl] = (kv[:, sl] + kr).astype(BF16)
        v_ref[...] = (kv[:, width:] + vone_ref[...]).astype(BF16)


def _mla_proj(x, g1, w_in_ext, g_q, g_kv, w_qb_ext, w_ukv_ext, tabs, v_one, with_kv):
    m, d = x.shape
    q_lora = g_q.shape[0]
    kv_lora = g_kv.shape[0]
    tm = _row_tile(m, 512)
    width = MLA_HEADS * HEAD_BLOCK
    n_pos = tabs[0].shape[0] // tm
    full = lambda a: pl.BlockSpec(a.shape, lambda i: (0,) * a.ndim)
    tab_spec = pl.BlockSpec((tm, LANES), lambda i: (i % n_pos, 0))
    row = lambda n: pl.BlockSpec((tm, n), lambda i: (i, 0))
    out_shape = [jax.ShapeDtypeStruct((m, width), BF16), jax.ShapeDtypeStruct((m, kv_lora), F32),
                 jax.ShapeDtypeStruct((m, LANES), F32)]
    out_specs = [row(width), row(kv_lora), row(LANES)]
    if with_kv:
        out_shape += [jax.ShapeDtypeStruct((m, width), BF16), jax.ShapeDtypeStruct((m, width), BF16)]
        out_specs += [row(width), row(width)]
    g1 = g1.reshape(1, d)
    g_q = g_q.reshape(1, q_lora)
    g_kv = g_kv.reshape(1, kv_lora)
    return pl.pallas_call(
        functools.partial(_mla_proj_kernel, q_lora=q_lora, kv_lora=kv_lora, with_kv=with_kv),
        grid=(m // tm,),
        in_specs=[row(d), full(g1), full(w_in_ext), full(g_q), full(g_kv), full(w_qb_ext), full(w_ukv_ext),
                  tab_spec, tab_spec, tab_spec, tab_spec, full(v_one)],
        out_specs=out_specs,
        out_shape=out_shape,
        compiler_params=_cparams("parallel"),
        name="mla_proj",
    )(x, g1, w_in_ext, g_q, g_kv, w_qb_ext, w_ukv_ext, *tabs, v_one)


def _mla_flash_kernel(q_ref, k_ref, v_ref, o_ref, m_sc, acc_sc, sa_sc, sb_sc, *, tq):
    qi = pl.program_id(2)
    tk = tq
    heads = (slice(0, HEAD_BLOCK), slice(HEAD_BLOCK, 2 * HEAD_BLOCK))
    m_sc[...] = jnp.full(m_sc.shape, NEG, F32)
    acc_sc[...] = jnp.zeros(acc_sc.shape, F32)

    def scores(kc, hh):
        start = pl.multiple_of(kc * tk, tk)
        return lax.dot_general(q_ref[:, heads[hh]], k_ref[pl.ds(start, tk), heads[hh]], NT_DIMS,
                               preferred_element_type=F32)

    def softmax_pv(kc, hh, s, masked):
        start = pl.multiple_of(kc * tk, tk)
        if masked:
            row = lax.broadcasted_iota(jnp.int32, (tq, tk), 0)
            col = lax.broadcasted_iota(jnp.int32, (tq, tk), 1)
            s = jnp.where(col <= row, s, NEG)
        m_prev = m_sc[hh]
        m_new = jnp.maximum(m_prev, jnp.max(s, axis=-1, keepdims=True))
        alpha = jnp.exp2(m_prev - m_new)
        p = jnp.exp2(s - jnp.concatenate([m_new] * (tk // LANES), axis=1))
        v = v_ref[pl.ds(start, tk), heads[hh]]
        acc_sc[hh] = acc_sc[hh] * alpha + jnp.dot(p.astype(BF16), v, preferred_element_type=F32)
        m_sc[hh] = m_new

    def chunk(kc, cur, nxt, masked):
        if nxt is not None:
            for hh in range(2):
                nxt[hh] = scores(kc + 1, hh)
        for hh in range(2):
            softmax_pv(kc, hh, cur[hh], masked)

    for hh in range(2):
        sa_sc[hh] = scores(0, hh)

    def body(pair, carry):
        chunk(2 * pair, sa_sc, sb_sc, False)
        chunk(2 * pair + 1, sb_sc, sa_sc, False)
        return carry

    lax.fori_loop(0, lax.shift_right_logical(qi, 1), body, 0)
    odd = (qi & 1) == 1

    @pl.when(odd)
    def _():
        chunk(qi - 1, sa_sc, sb_sc, False)
        chunk(qi, sb_sc, None, True)

    @pl.when(jnp.logical_not(odd))
    def _():
        chunk(qi, sa_sc, None, True)

    lane = lax.broadcasted_iota(jnp.int32, (tq, HEAD_BLOCK), 1)
    acc0 = acc_sc[0]
    acc1 = acc_sc[1]
    out0 = acc0 / acc0[:, V_DIM:V_DIM + 1]
    out1 = acc1 / acc1[:, 0:1]
    o_ref[...] = jnp.where(lane < V_DIM, out0, out1).astype(o_ref.dtype)


def _mla_flash(q, k, v, batch, seq):
    width = MLA_HEADS * HEAD_BLOCK
    q = q.reshape(batch, seq, width)
    k = k.reshape(batch, seq, width)
    v = v.reshape(batch, seq, width)
    tq = _row_tile(seq, 512)
    pair = 2 * HEAD_BLOCK
    out = pl.pallas_call(
        functools.partial(_mla_flash_kernel, tq=tq),
        grid=(batch, MLA_HEADS // 2, seq // tq),
        in_specs=[
            pl.BlockSpec((None, tq, pair), lambda b, h, i: (b, i, h)),
            pl.BlockSpec((None, seq, pair), lambda b, h, i: (b, 0, h)),
            pl.BlockSpec((None, seq, pair), lambda b, h, i: (b, 0, h)),
        ],
        out_specs=pl.BlockSpec((None, tq, 2 * V_DIM), lambda b, h, i: (b, i, h)),
        out_shape=jax.ShapeDtypeStruct((batch, seq, MLA_HEADS * V_DIM), BF16),
        scratch_shapes=[pltpu.VMEM((2, tq, HEAD_BLOCK), F32), pltpu.VMEM((2, tq, HEAD_BLOCK), F32),
                        pltpu.VMEM((2, tq, tq), F32), pltpu.VMEM((2, tq, tq), F32)],
        compiler_params=_cparams("parallel", "parallel", "arbitrary"),
        name="mla_flash",
    )(q, k, v)
    return out.reshape(batch * seq, MLA_HEADS * V_DIM)


def _mla_decode_kernel(pt_ref, ql_ref, qr_ref, newt_ref, *rest, chain_tiles, page, kv_lora, t_new):
    n_pages = sum(chain_tiles) - 1
    page_refs = rest[:n_pages]
    o_ref = rest[n_pages]
    scratch = rest[n_pages + 1:]
    rows = ql_ref.shape[0]
    tiles = list(page_refs) + [newt_ref]
    ql = ql_ref[...]
    qr = qr_ref[...]
    stats = []
    first = 0
    for g, n_tiles in enumerate(chain_tiles):
        lat_sc, kr_sc = scratch[2 * g], scratch[2 * g + 1]
        for i, ref in enumerate(tiles[first:first + n_tiles]):
            lat_sc[:, i * page:(i + 1) * page] = ref[:kv_lora, :].astype(BF16)
            kr_sc[:, i * page:(i + 1) * page] = ref[kv_lora:, :].astype(BF16)
        first += n_tiles
        s = (jnp.dot(ql, lat_sc[...], preferred_element_type=F32)
             + jnp.dot(qr, kr_sc[...], preferred_element_type=F32))
        if g == len(chain_tiles) - 1:
            n_keys = n_tiles * page
            col = lax.broadcasted_iota(jnp.int32, (rows, n_keys), 1)
            t_row = lax.shift_right_logical(lax.broadcasted_iota(jnp.int32, (rows, n_keys), 0),
                                            int(math.log2(MLA_HEADS)))
            j_new = col - (n_tiles - 1) * page
            s = jnp.where((j_new < 0) | ((j_new <= t_row) & (j_new < t_new)), s, NEG)
        m = jnp.max(s, axis=-1, keepdims=True)
        p = jnp.exp(s - m)
        l = jnp.sum(p, axis=-1, keepdims=True)
        pv = lax.dot_general(p.astype(BF16), lat_sc[...], NT_DIMS, preferred_element_type=F32)
        stats.append((m, l, pv))
    m_all = functools.reduce(jnp.maximum, [st[0] for st in stats])
    num = den = None
    for m, l, pv in stats:
        a = jnp.exp(m - m_all)
        num = a * pv if num is None else num + a * pv
        den = a * l if den is None else den + a * l
    o_ref[...] = (num / den).astype(o_ref.dtype)


def _mla_decode(q_lat, q_rope, new_t, cache_t, layer, page_table):
    bd, rows, kv_lora = q_lat.shape
    n_pages = page_table.shape[1]
    row_w, page = cache_t.shape[2], cache_t.shape[3]
    t_new = rows // MLA_HEADS
    n_chains = 4 if n_pages >= 8 else 1
    base, extra = divmod(n_pages + 1, n_chains)
    chain_tiles = tuple(base + (1 if g >= n_chains - extra else 0) for g in range(n_chains))

    def page_spec(i):
        return pl.BlockSpec((None, None, row_w, page), lambda b, pt: (layer, pt[b, i], 0, 0))

    scratch = []
    for n_tiles in chain_tiles:
        scratch += [pltpu.VMEM((kv_lora, n_tiles * page), BF16), pltpu.VMEM((QK_ROPE, n_tiles * page), BF16)]
    grid_spec = pltpu.PrefetchScalarGridSpec(
        num_scalar_prefetch=1,
        grid=(bd,),
        in_specs=[
            pl.BlockSpec((None, rows, kv_lora), lambda b, pt: (b, 0, 0)),
            pl.BlockSpec((None, rows, QK_ROPE), lambda b, pt: (b, 0, 0)),
            pl.BlockSpec((None, row_w, page), lambda b, pt: (b, 0, 0)),
        ] + [page_spec(i) for i in range(n_pages)],
        out_specs=pl.BlockSpec((None, rows, kv_lora), lambda b, pt: (b, 0, 0)),
        scratch_shapes=scratch,
    )
    kern = functools.partial(_mla_decode_kernel, chain_tiles=chain_tiles, page=page, kv_lora=kv_lora, t_new=t_new)
    return pl.pallas_call(
        kern,
        grid_spec=grid_spec,
        out_shape=jax.ShapeDtypeStruct((bd, rows, kv_lora), BF16),
        compiler_params=_cparams("parallel"),
        name="mla_decode",
    )(page_table, q_lat, q_rope, new_t, *([cache_t] * n_pages))


def _norm_rope_kernel(x_ref, g_ref, w_ref, c_ref, s1_ref, s2_ref, *refs, rope_blocks, group_dils, group_blocks,
                      n_f32_out):
    n_groups = 0 if group_dils is None else len(group_dils)
    n_out = (n_groups if group_dils is not None else 0) + n_f32_out
    out_refs = refs[:n_out]
    sc = refs[n_out] if group_dils is not None else None
    tm = x_ref.shape[0]
    h = _rms(x_ref[...], g_ref[...]).astype(BF16)
    y = jnp.dot(h, w_ref[...], preferred_element_type=F32)
    ct, s1, s2 = c_ref[...], s1_ref[...], s2_ref[...]
    half = ROT_DIM // 2
    for j, roped in enumerate(rope_blocks):
        sl = slice(j * LANES, (j + 1) * LANES)
        blk = y[:, sl]
        if roped:
            blk = blk * ct + pltpu.roll(blk, LANES - half, 1) * s1 + pltpu.roll(blk, half, 1) * s2
        if n_f32_out:
            out_refs[n_out - 1][:, sl] = blk
        if group_dils is not None:
            gi, jj = divmod(j, group_blocks)
            dil = group_dils[gi]
            width = group_blocks * LANES
            if dil == 1:
                out_refs[gi][:, jj * LANES:(jj + 1) * LANES] = blk.astype(BF16)
            else:
                sc[...] = blk
                for r in range(dil):
                    lo = r * width + jj * LANES
                    out_refs[gi][:, lo:lo + LANES] = sc[pl.ds(r, tm // dil, stride=dil), :].astype(BF16)


def _norm_rope(x, g, w, tabs, rope_blocks, *, group_dils=None, group_blocks=None, emit_f32=False):
    m, d = x.shape
    n = w.shape[1]
    tm = _row_tile(m, 512)
    n_pos = tabs[0].shape[0] // tm
    tab_spec = pl.BlockSpec((tm, LANES), lambda i: (i % n_pos, 0))
    out_shape, out_specs, scratch = [], [], []
    if group_dils is not None:
        width = group_blocks * LANES
        for dil in group_dils:
            assert tm % (dil * 2 * SUBLANES) == 0
            out_shape.append(jax.ShapeDtypeStruct((m // dil, dil * width), BF16))
            out_specs.append(pl.BlockSpec((tm // dil, dil * width), lambda i: (i, 0)))
        scratch = [pltpu.VMEM((tm, LANES), F32)]
    else:
        emit_f32 = True
    if emit_f32:
        out_shape.append(jax.ShapeDtypeStruct((m, n), F32))
        out_specs.append(pl.BlockSpec((tm, n), lambda i: (i, 0)))
    return pl.pallas_call(
        functools.partial(_norm_rope_kernel, rope_blocks=tuple(rope_blocks), group_dils=group_dils,
                          group_blocks=group_blocks, n_f32_out=int(emit_f32)),
        grid=(m // tm,),
        in_specs=[pl.BlockSpec((tm, d), lambda i: (i, 0)), pl.BlockSpec((1, d), lambda i: (0, 0)),
                  pl.BlockSpec((d, n), lambda i: (0, 0)), tab_spec, tab_spec, tab_spec],
        out_specs=out_specs,
        out_shape=out_shape,
        scratch_shapes=scratch,
        compiler_params=_cparams("parallel"),
        name="norm_rope",
    )(x, g.reshape(1, d), w, *tabs)


def _dil_prompt_kernel(q_ref, kp_ref, kc_ref, vp_ref, vc_ref, o_ref, lse_ref, *, tq, n_sub):
    cblk = pl.program_id(2)
    lane = lax.broadcasted_iota(jnp.int32, (tq, LANES), 1)
    i_idx = lax.broadcasted_iota(jnp.int32, (2 * tq, 2 * tq), 0) & (tq - 1)
    j_idx = lax.broadcasted_iota(jnp.int32, (2 * tq, 2 * tq), 1)
    valid_cur = (j_idx >= tq) & (j_idx - tq <= i_idx)
    valid_mid = ((j_idx < tq) & (j_idx >= i_idx)) | valid_cur
    prev_off = jnp.where(cblk > 0, 0, tq)
    valid_first = ((j_idx < tq) & (j_idx >= i_idx + prev_off)) | valid_cur
    for sub in range(n_sub):
        rs = slice(sub * tq, (sub + 1) * tq)
        ps = slice((sub - 1) * tq, sub * tq)
        valid = valid_first if sub == 0 else valid_mid
        for hp in range(HEADS_PER_GROUP // 2):
            sl = slice(hp * LANES, (hp + 1) * LANES)
            q2 = q_ref[rs, sl]
            zero = jnp.zeros_like(q2)
            qs = jnp.concatenate([jnp.where(lane < HEAD_DIM, q2, zero), jnp.where(lane >= HEAD_DIM, q2, zero)],
                                 axis=0)
            k_prev = kp_ref[:, sl] if sub == 0 else kc_ref[ps, sl]
            v_prev = vp_ref[:, sl] if sub == 0 else vc_ref[ps, sl]
            kcat = jnp.concatenate([k_prev, kc_ref[rs, sl]], axis=0)
            vcat = jnp.concatenate([v_prev, vc_ref[rs, sl]], axis=0)
            s = jnp.where(valid, lax.dot_general(qs, kcat, NT_DIMS, preferred_element_type=F32), NEG)
            m = jnp.max(s, axis=-1, keepdims=True)
            p = jnp.exp(s - m)
            den = jnp.sum(p, axis=-1, keepdims=True)
            o2 = jnp.dot(p.astype(BF16), vcat, preferred_element_type=F32) / den
            lse2 = jnp.broadcast_to(m + jnp.log(den), (2 * tq, LANES))
            o_ref[rs, sl] = jnp.where(lane < HEAD_DIM, o2[:tq], o2[tq:])
            lse_ref[rs, sl] = jnp.where(lane < HEAD_DIM, lse2[:tq], lse2[tq:])


def _dil_prompt_group(q_g, kv_g, dil, batch, seq):
    assert seq % (dil * DIL_N) == 0
    rows = seq // dil
    tq = DIL_N
    n_sub = min(4, rows // tq)
    nblk = rows // (tq * n_sub)
    qv = q_g.reshape(batch, rows, dil * GROUP_WIDTH)
    kvv = kv_g.reshape(batch, rows, dil * 2 * GROUP_WIDTH)
    blk = (None, tq * n_sub, GROUP_WIDTH)
    pblk = (None, tq, GROUP_WIDTH)
    cur = lambda off: pl.BlockSpec(blk, lambda b, r, c: (b, c, 2 * r + off))
    prev = lambda off: pl.BlockSpec(pblk, lambda b, r, c: (b, jnp.maximum(c * n_sub - 1, 0), 2 * r + off))
    rspec = pl.BlockSpec(blk, lambda b, r, c: (b, c, r))
    o, lse = pl.pallas_call(
        functools.partial(_dil_prompt_kernel, tq=tq, n_sub=n_sub),
        grid=(batch, dil, nblk),
        in_specs=[rspec, prev(0), cur(0), prev(1), cur(1)],
        out_specs=[rspec, rspec],
        out_shape=[jax.ShapeDtypeStruct((batch, rows, dil * GROUP_WIDTH), F32)] * 2,
        compiler_params=_cparams("parallel", "parallel", "arbitrary"),
        name="dil_prompt",
    )(qv, kvv, kvv, kvv, kvv)
    return o.reshape(batch * rows, dil * GROUP_WIDTH), lse.reshape(batch * rows, dil * GROUP_WIDTH)


def _dil_sample_kernel(q_ref, buf_ref, new_ref, *rest, bb, dil, t_new, emit_state):
    if emit_state:
        newt_ref, o_ref, lse_ref, state_ref = rest
    else:
        o_ref, lse_ref = rest
    w = buf_ref.shape[2]
    rows = t_new * HEADS_PER_GROUP
    log_h = int(math.log2(HEADS_PER_GROUP))
    lane = lax.broadcasted_iota(jnp.int32, (rows, GROUP_WIDTH), 1)
    row = lax.broadcasted_iota(jnp.int32, (rows, GROUP_WIDTH), 0)
    diag = lax.shift_right_logical(lane, int(math.log2(HEAD_DIM))) == (row & (HEADS_PER_GROUP - 1))
    w_idx = lax.broadcasted_iota(jnp.int32, (rows, w), 1)
    t_of_row = lax.shift_right_logical(lax.broadcasted_iota(jnp.int32, (rows, w), 0), log_h)
    valid = (w_idx >= t_of_row) if dil == 1 else ((w_idx & (dil - 1)) == t_of_row)
    n_new = new_ref.shape[1]
    j_new = lax.broadcasted_iota(jnp.int32, (rows, n_new), 1)
    t_new_row = lax.shift_right_logical(lax.broadcasted_iota(jnp.int32, (rows, n_new), 0), log_h)
    valid_new = (j_new <= t_new_row) if dil == 1 else (j_new == t_new_row)
    shift_lane = lax.broadcasted_iota(jnp.int32, (2 * GROUP_WIDTH, LANES), 1)
    for b in range(bb):
        q = q_ref[b]
        qrep = jnp.concatenate(
            [jnp.broadcast_to(q[t:t + 1, :], (HEADS_PER_GROUP, GROUP_WIDTH)) for t in range(t_new)], axis=0)
        qbd = jnp.where(diag, qrep, 0.0).astype(BF16)
        kt = buf_ref[b, :GROUP_WIDTH, :].astype(BF16)
        vt = buf_ref[b, GROUP_WIDTH:, :].astype(BF16)
        k_new = new_ref[b, :, :GROUP_WIDTH].astype(BF16)
        v_new = new_ref[b, :, GROUP_WIDTH:].astype(BF16)
        s = jnp.where(valid, jnp.dot(qbd, kt, preferred_element_type=F32), NEG)
        s_new = jnp.where(valid_new, lax.dot_general(qbd, k_new, NT_DIMS, preferred_element_type=F32), NEG)
        m = jnp.maximum(jnp.max(s, axis=-1, keepdims=True), jnp.max(s_new, axis=-1, keepdims=True))
        p = jnp.exp(s - m)
        p_new = jnp.exp(s_new - m)
        den = jnp.sum(p, axis=-1, keepdims=True) + jnp.sum(p_new, axis=-1, keepdims=True)
        o = (lax.dot_general(p.astype(BF16), vt, NT_DIMS, preferred_element_type=F32)
             + jnp.dot(p_new.astype(BF16), v_new, preferred_element_type=F32)) / den
        lse = jnp.broadcast_to(m + jnp.log(den), (rows, GROUP_WIDTH))
        o_ref[b] = jnp.sum(jnp.where(diag, o, 0.0).reshape(t_new, HEADS_PER_GROUP, GROUP_WIDTH), axis=1)
        lse_ref[b] = jnp.sum(jnp.where(diag, lse, 0.0).reshape(t_new, HEADS_PER_GROUP, GROUP_WIDTH), axis=1)
        if emit_state:
            n_col = w // LANES
            nxt = pltpu.roll(buf_ref[b, :, 0:LANES], LANES - t_new, 1)
            for jc in range(n_col):
                cur = nxt
                if jc + 1 < n_col:
                    nxt = pltpu.roll(buf_ref[b, :, (jc + 1) * LANES:(jc + 2) * LANES], LANES - t_new, 1)
                    cur = jnp.where(shift_lane < LANES - t_new, cur, nxt)
                state_ref[b, :, jc * LANES:(jc + 1) * LANES] = cur
            state_ref[b, :, w - t_new:w] = newt_ref[b][:, :t_new]


def _dil_sample_group(q_g, buf_t, new_g, new_t, dil, emit_state):
    bd, t_new, _ = q_g.shape
    w = buf_t.shape[2]
    assert w == dil * DIL_N, "state buffer must hold exactly one window"
    assert dil == 1 or dil >= t_new
    kv_w = 2 * GROUP_WIDTH
    bb = _row_tile(bd, max(1, 1024 // w))
    spec3 = lambda a: pl.BlockSpec((bb,) + a.shape[1:], lambda i: (i, 0, 0))
    in_specs = [spec3(q_g), spec3(buf_t), spec3(new_g)]
    args = [q_g, buf_t, new_g]
    out_spec = pl.BlockSpec((bb, t_new, GROUP_WIDTH), lambda i: (i, 0, 0))
    out_specs = [out_spec, out_spec]
    out_shape = [jax.ShapeDtypeStruct((bd, t_new, GROUP_WIDTH), F32)] * 2
    if emit_state:
        in_specs.append(spec3(new_t))
        args.append(new_t)
        out_specs.append(spec3(buf_t))
        out_shape.append(jax.ShapeDtypeStruct(buf_t.shape, F32))
    res = pl.pallas_call(
        functools.partial(_dil_sample_kernel, bb=bb, dil=dil, t_new=t_new, emit_state=emit_state),
        grid=(bd // bb,),
        in_specs=in_specs,
        out_specs=out_specs,
        out_shape=out_shape,
        compiler_params=_cparams("parallel"),
        name="dil_sample",
    )(*args)
    o, lse = res[0].reshape(bd * t_new, GROUP_WIDTH), res[1].reshape(bd * t_new, GROUP_WIDTH)
    return o, lse, (res[2] if emit_state else None)


def _dil_out_kernel(*refs, group_dils):
    n_g = N_GROUPS
    o_refs, l_refs = refs[:n_g], refs[n_g:2 * n_g]
    w_ref, x_ref, y_ref = refs[2 * n_g:2 * n_g + 3]
    scratch = refs[2 * n_g + 3:]
    tm = x_ref.shape[0]
    cols = []
    for jj in range(GROUP_LANE_BLOCKS):
        sl = slice(jj * LANES, (jj + 1) * LANES)
        os_, ls_ = [], []
        for gi in range(n_g):
            dil = 1 if group_dils is None else group_dils[gi]
            if dil == 1:
                os_.append(o_refs[gi][:, sl])
                ls_.append(l_refs[gi][:, sl])
            else:
                for src, dst in ((o_refs[gi], scratch[0]), (l_refs[gi], scratch[1])):
                    for r in range(dil):
                        lo = r * GROUP_WIDTH + jj * LANES
                        dst[pl.ds(r, tm // dil, stride=dil), :] = src[:, lo:lo + LANES]
                os_.append(scratch[0][...])
                ls_.append(scratch[1][...])
        m = jnp.maximum(jnp.maximum(ls_[0], ls_[1]), ls_[2])
        e = [jnp.exp(l - m) for l in ls_]
        den = e[0] + e[1] + e[2]
        cols.append(((e[0] / den) * os_[0] + (e[1] / den) * os_[1] + (e[2] / den) * os_[2]).astype(BF16))
    o = jnp.concatenate(cols, axis=1)
    y_ref[...] = x_ref[...] + jnp.dot(o, w_ref[...], preferred_element_type=F32)


def _dil_out(outs, lses, w_o, x, group_dils=None):
    m, d = x.shape
    tm = _row_tile(m, 512)
    specs = []
    for gi in range(N_GROUPS):
        dil = 1 if group_dils is None else group_dils[gi]
        specs.append(pl.BlockSpec((tm // dil, dil * GROUP_WIDTH), lambda i: (i, 0)))
    scratch = [] if group_dils is None else [pltpu.VMEM((tm, LANES), F32)] * 2
    return pl.pallas_call(
        functools.partial(_dil_out_kernel, group_dils=group_dils),
        grid=(m // tm,),
        in_specs=specs + specs + [pl.BlockSpec(w_o.shape, lambda i: (0, 0)), pl.BlockSpec((tm, d), lambda i: (i, 0))],
        out_specs=pl.BlockSpec((tm, d), lambda i: (i, 0)),
        out_shape=jax.ShapeDtypeStruct((m, d), F32),
        scratch_shapes=scratch,
        compiler_params=_cparams("parallel"),
        name="dil_out",
    )(*outs, *lses, w_o, x)


def _rope_cos_sin(pos, rot_dim, theta):
    half = rot_dim // 2
    inv = jnp.float32(theta) ** (-2.0 * jnp.arange(half, dtype=F32) / rot_dim)
    ang = pos[:, None] * inv[None, :]
    return jnp.cos(ang), jnp.sin(ang)


def _mla_tables(pos, q_scale):
    cos, sin = _rope_cos_sin(pos, QK_ROPE, MLA_THETA)
    n = pos.shape[0]
    z = lambda w: jnp.zeros((n, w), F32)
    cos2 = jnp.concatenate([cos, cos], axis=1)
    sin2 = jnp.concatenate([sin, sin], axis=1)
    rest = HEAD_BLOCK - QK_NOPE - QK_ROPE
    cq = jnp.concatenate([jnp.ones((n, QK_NOPE), F32), cos2, z(rest)], axis=1) * q_scale
    sq = jnp.concatenate([z(QK_NOPE), sin2, z(rest)], axis=1) * q_scale
    ck = jnp.concatenate([z(QK_NOPE), cos2, z(rest)], axis=1)
    sk = jnp.concatenate([z(QK_NOPE), sin2, z(rest)], axis=1)
    return cq, sq, ck, sk


def _dil_tables(pos, scale):
    cos, sin = _rope_cos_sin(pos, ROT_DIM, ROPE_THETA)
    n = pos.shape[0]
    half = ROT_DIM // 2
    z = lambda w: jnp.zeros((n, w), F32)
    keep = jnp.ones((n, HEAD_DIM - ROT_DIM), F32)
    c_head = jnp.concatenate([cos, cos, keep], axis=1)
    s1_head = jnp.concatenate([-sin, z(HEAD_DIM - half)], axis=1)
    s2_head = jnp.concatenate([z(half), sin, z(HEAD_DIM - ROT_DIM)], axis=1)
    two = lambda a: jnp.concatenate([a, a], axis=1) * scale
    return two(c_head), two(s1_head), two(s2_head)


def _rot_half_cols(w):
    half = w.shape[-1] // 2
    return jnp.concatenate([-w[..., half:], w[..., :half]], axis=-1)


def _prep_mla_weights(w_in, w_qb, w_ukv, q_lora, kv_lora):
    d = w_in.shape[0]
    w_kr = w_in[:, q_lora + kv_lora:]
    assert HEAD_BLOCK - QK_NOPE - QK_ROPE == QK_ROPE
    w_in_ext = jnp.concatenate(
        [w_in[:, :q_lora + kv_lora], jnp.zeros((d, QK_NOPE), F32), w_kr, _rot_half_cols(w_kr)], axis=1)
    wq = w_qb.reshape(q_lora, MLA_HEADS, QK_NOPE + QK_ROPE)
    wq_rope = wq[..., QK_NOPE:]
    w_qb_ext = jnp.concatenate([wq[..., :QK_NOPE], wq_rope, _rot_half_cols(wq_rope)], axis=-1)
    w_qb_ext = w_qb_ext.reshape(q_lora, MLA_HEADS * HEAD_BLOCK)
    wkv = w_ukv.reshape(kv_lora, MLA_HEADS, QK_NOPE + V_DIM)
    zpad = jnp.zeros((kv_lora, MLA_HEADS, HEAD_BLOCK - QK_NOPE), F32)
    wk = jnp.concatenate([wkv[..., :QK_NOPE], zpad], axis=-1).reshape(kv_lora, MLA_HEADS * HEAD_BLOCK)
    wv_even = jnp.concatenate([wkv[..., QK_NOPE:], zpad], axis=-1)
    wv_odd = jnp.concatenate([zpad, wkv[..., QK_NOPE:]], axis=-1)
    odd = (jnp.arange(MLA_HEADS) % 2 == 1)[None, :, None]
    wv = jnp.where(odd, wv_odd, wv_even).reshape(kv_lora, MLA_HEADS * HEAD_BLOCK)
    w_ukv_ext = jnp.concatenate([wk, wv], axis=1)
    one = np.zeros((1, MLA_HEADS, HEAD_BLOCK), np.float32)
    one[0, 0::2, V_DIM] = 1.0
    one[0, 1::2, 0] = 1.0
    v_one = jnp.asarray(one.reshape(1, MLA_HEADS * HEAD_BLOCK))
    eye = jnp.eye(MLA_HEADS, dtype=F32)
    w_uk = wkv[..., :QK_NOPE]
    w_uk_rows = jnp.concatenate([jnp.transpose(w_uk, (1, 2, 0)),
                                 jnp.zeros((MLA_HEADS, HEAD_BLOCK - QK_NOPE, kv_lora), F32)], axis=1)
    w_uk_bd = jnp.einsum('hnc,hg->hngc', w_uk_rows, eye).reshape(MLA_HEADS * HEAD_BLOCK, MLA_HEADS * kv_lora)
    w_uv = jnp.transpose(wkv[..., QK_NOPE:], (1, 0, 2))
    w_uv_bd = jnp.einsum('hcv,hg->hcgv', w_uv, eye).reshape(MLA_HEADS * kv_lora, MLA_HEADS * V_DIM)
    bf = lambda a: a.astype(BF16)
    return bf(w_in_ext), bf(w_qb_ext), bf(w_ukv_ext), v_one, bf(w_uk_bd), bf(w_uv_bd)


def _group_major_kv_cols(w_shared):
    d = w_shared.shape[0]
    w = w_shared.reshape(d, 2, N_GROUPS, GROUP_WIDTH)
    return jnp.transpose(w, (0, 2, 1, 3)).reshape(d, 2 * DIL_WIDTH)


def kernel(x_prompt, x_sample, cache_mla, page_table, state_dil_w128, state_dil_w512, state_dil_w2048,
           g_layers, w_ffn_in, w_ffn_out, w_mla_in, g_mla_q, g_mla_kv, w_mla_qb, w_mla_ukv, w_mla_o,
           g_shared_kv, w_shared_kv, w_dil_q, w_dil_o, g_final):
    batch, seq, d_model = x_prompt.shape
    bd, t_new, _ = x_sample.shape
    depth = g_layers.shape[0]
    n_a = w_mla_in.shape[0]
    q_lora = g_mla_q.shape[1]
    kv_lora = g_mla_kv.shape[1]
    page = cache_mla.shape[2]
    past_len = page_table.shape[1] * page
    buffers = [state_dil_w128, state_dil_w512, state_dil_w2048]
    dils = tuple(dil for _, dil in DIL_GROUPS)
    kv_w = 2 * GROUP_WIDTH

    w_ffn_in_b = w_ffn_in.astype(BF16)
    w_ffn_out_b = w_ffn_out.astype(BF16)
    mla_w = [_prep_mla_weights(w_mla_in[a], w_mla_qb[a], w_mla_ukv[a], q_lora, kv_lora) for a in range(n_a)]
    w_mla_o_b = w_mla_o.astype(BF16)
    w_shared_b = _group_major_kv_cols(w_shared_kv).astype(BF16)
    w_dil_q_b = w_dil_q.astype(BF16)
    w_dil_o_b = w_dil_o.astype(BF16)
    kv_rope_blocks = []
    for _ in range(N_GROUPS):
        kv_rope_blocks += [True] * GROUP_LANE_BLOCKS + [False] * GROUP_LANE_BLOCKS
    q_rope_blocks = [True] * (DIL_WIDTH // LANES)

    pos_p = jnp.arange(seq, dtype=F32)
    pos_s = jnp.tile((past_len + jnp.arange(t_new)).astype(F32), bd)

    def ffn(x, l, half, final_norm=False):
        return _ffn(x, g_layers[l, 2 * half], w_ffn_in_b, w_ffn_out_b, l, half, g_final, final_norm=final_norm)

    mla_tabs_p = _mla_tables(pos_p, MLA_SCALE * LOG2E)
    dil_q_tabs_p = _dil_tables(pos_p, DIL_SCALE)
    dil_k_tabs_p = _dil_tables(pos_p, 1.0)
    x = x_prompt.reshape(batch * seq, d_model)
    rows_p = []
    kv_groups = kv_f32_p = None
    for l in range(depth):
        x = ffn(x, l, 0)
        if l < n_a:
            w_in_ext, w_qb_ext, w_ukv_ext, v_one, _, _ = mla_w[l]
            q, ckv, kr, k, v = _mla_proj(x, g_layers[l, 1], w_in_ext, g_mla_q[l], g_mla_kv[l], w_qb_ext, w_ukv_ext,
                                         mla_tabs_p, v_one, with_kv=True)
            o = _mla_flash(q, k, v, batch, seq)
            x = _linear(o, w_mla_o_b[l], residual=x)
            rows_p.append(jnp.concatenate([ckv, kr[:, QK_NOPE:QK_NOPE + QK_ROPE]], axis=1))
        else:
            q_groups = _norm_rope(x, g_layers[l, 1], w_dil_q_b[l - n_a], dil_q_tabs_p, q_rope_blocks,
                                  group_dils=dils, group_blocks=GROUP_LANE_BLOCKS)
            res = [_dil_prompt_group(q_groups[gi], kv_groups[gi], dil, batch, seq) for gi, dil in enumerate(dils)]
            x = _dil_out([r[0] for r in res], [r[1] for r in res], w_dil_o_b[l - n_a], x, group_dils=dils)
        x = ffn(x, l, 1, final_norm=(l == depth - 1))
        if l == n_a - 1:
            *kv_groups, kv_f32_p = _norm_rope(x, g_shared_kv, w_shared_b, dil_k_tabs_p, kv_rope_blocks,
                                              group_dils=dils, group_blocks=2 * GROUP_LANE_BLOCKS, emit_f32=True)
    y_prompt = x.reshape(batch, seq, d_model)
    mla_rows_prompt = jnp.stack(rows_p, axis=0).reshape(n_a, batch, seq, kv_lora + QK_ROPE)
    kv_p3 = kv_f32_p.reshape(batch, seq, N_GROUPS * kv_w)
    dil_p = []
    for gi, (win, _) in enumerate(DIL_GROUPS):
        keep = min(win, seq)
        tail = kv_p3[:, seq - keep:, gi * kv_w:(gi + 1) * kv_w]
        dil_p.append(tail.reshape(batch, keep, 2, HEADS_PER_GROUP, HEAD_DIM))

    mla_tabs_s = _mla_tables(pos_s, MLA_SCALE)
    dil_q_tabs_s = _dil_tables(pos_s, DIL_SCALE)
    dil_k_tabs_s = _dil_tables(pos_s, 1.0)
    cache_t = jnp.swapaxes(cache_mla, 2, 3)
    bufs_t = []
    for buf in buffers:
        length = buf.shape[1]
        assert min(length, length + t_new) == length
        bufs_t.append(jnp.transpose(buf, (0, 2, 3, 4, 1)).reshape(bd, kv_w, length))
    new_rows_pad = SUBLANES
    x = x_sample.reshape(bd * t_new, d_model)
    rows_s = []
    kv_new = kv_new_t = None
    states_t = [None] * N_GROUPS
    for l in range(depth):
        x = ffn(x, l, 0)
        if l < n_a:
            w_in_ext, w_qb_ext, w_ukv_ext, v_one, w_uk_bd, w_uv_bd = mla_w[l]
            q, ckv, kr = _mla_proj(x, g_layers[l, 1], w_in_ext, g_mla_q[l], g_mla_kv[l], w_qb_ext, w_ukv_ext,
                                   mla_tabs_s, v_one, with_kv=False)
            rows = jnp.concatenate([ckv, kr[:, QK_NOPE:QK_NOPE + QK_ROPE]], axis=1)
            rows_s.append(rows)
            q_lat = _linear(q, w_uk_bd, out_dtype=BF16).reshape(bd, t_new * MLA_HEADS, kv_lora)
            q_rope = q.reshape(bd, t_new * MLA_HEADS, HEAD_BLOCK)[:, :, QK_NOPE:QK_NOPE + QK_ROPE]
            new_pad = jnp.pad(rows.reshape(bd, t_new, kv_lora + QK_ROPE), ((0, 0), (0, page - t_new), (0, 0)))
            o_lat = _mla_decode(q_lat, q_rope, jnp.swapaxes(new_pad, 1, 2), cache_t, l, page_table)
            o = _linear(o_lat.reshape(bd * t_new, MLA_HEADS * kv_lora), w_uv_bd, out_dtype=BF16)
            x = _linear(o, w_mla_o_b[l], residual=x)
        else:
            q = _norm_rope(x, g_layers[l, 1], w_dil_q_b[l - n_a], dil_q_tabs_s, q_rope_blocks)[0]
            qg = q.reshape(bd, t_new, N_GROUPS, GROUP_WIDTH)
            outs, lses = [], []
            for gi, dil in enumerate(dils):
                emit = l == n_a
                o, lse, st = _dil_sample_group(qg[:, :, gi], bufs_t[gi], kv_new[:, :, gi], kv_new_t[:, gi], dil, emit)
                if emit:
                    states_t[gi] = st
                outs.append(o)
                lses.append(lse)
            x = _dil_out(outs, lses, w_dil_o_b[l - n_a], x)
        x = ffn(x, l, 1, final_norm=(l == depth - 1))
        if l == n_a - 1:
            kv_f32_s = _norm_rope(x, g_shared_kv, w_shared_b, dil_k_tabs_s, kv_rope_blocks)[0]
            kv_s = kv_f32_s.reshape(bd, t_new, N_GROUPS, kv_w)
            kv_new = jnp.pad(kv_s, ((0, 0), (0, new_rows_pad - t_new), (0, 0), (0, 0)))
            kv_new_t = jnp.transpose(kv_new, (0, 2, 3, 1))
    y_sample = x.reshape(bd, t_new, d_model)
    mla_rows_sample = jnp.stack(rows_s, axis=0).reshape(n_a, bd, t_new, kv_lora + QK_ROPE)
    dil_s = []
    for gi, buf in enumerate(buffers):
        length = buf.shape[1]
        st = states_t[gi].reshape(bd, 2, HEADS_PER_GROUP, HEAD_DIM, length)
        dil_s.append(jnp.transpose(st, (0, 4, 1, 2, 3)))

    return (y_prompt, y_sample, mla_rows_prompt, mla_rows_sample,
            dil_p[0], dil_p[1], dil_p[2], dil_s[0], dil_s[1], dil_s[2])
```

```python
import functools
import math

import numpy as np
import jax
import jax.numpy as jnp
from jax import lax
from jax.experimental import pallas as pl
from jax.experimental.pallas import tpu as pltpu

F32 = jnp.float32
BF16 = jnp.bfloat16

MLA_HEADS = 16
QK_NOPE = 64
QK_ROPE = 32
V_DIM = 64
MLA_THETA = 10000.0
MLA_SCALE = (QK_NOPE + QK_ROPE) ** -0.5
DIL_GROUPS = ((128, 1), (512, 4), (2048, 16))
N_GROUPS = len(DIL_GROUPS)
HEADS_PER_GROUP = 8
HEAD_DIM = 64
GROUP_WIDTH = HEADS_PER_GROUP * HEAD_DIM
DIL_WIDTH = N_GROUPS * GROUP_WIDTH
ROT_DIM = HEAD_DIM // 4
ROPE_THETA = 500000.0
DIL_SCALE = HEAD_DIM ** -0.5
DIL_N = 128
EPS = 1e-6
NEG = -1e30
LOG2E = math.log2(math.e)

LANES = 128
SUBLANES = 8
HEAD_BLOCK = 128
GROUP_LANE_BLOCKS = GROUP_WIDTH // LANES
MXU_DIM = 256
FFN_SUB_COLS = 2 * MXU_DIM
VMEM_LIMIT_BYTES = 56 * 1024 * 1024
NT_DIMS = (((1,), (1,)), ((), ()))


def _cparams(*sem):
    return pltpu.CompilerParams(dimension_semantics=sem, vmem_limit_bytes=VMEM_LIMIT_BYTES)


def _row_tile(m, pref):
    t = min(m, pref)
    assert m % t == 0, (m, t)
    return t


def _rms(x, g):
    return x * lax.rsqrt(jnp.mean(x * x, axis=-1, keepdims=True) + EPS) * g


def _ffn_kernel(x_ref, g_ref, wg_ref, wu_ref, wo_ref, gf_ref, o_ref, *, final_norm, sub):
    x = x_ref[...]
    h = _rms(x, g_ref[...]).astype(BF16)
    d_ff = wg_ref.shape[1]
    part = None
    for lo in range(0, d_ff, sub):
        hi = min(lo + sub, d_ff)
        gate = jnp.dot(h, wg_ref[:, lo:hi], preferred_element_type=F32)
        up = jnp.dot(h, wu_ref[:, lo:hi], preferred_element_type=F32)
        act = (gate / (1.0 + jnp.exp(-gate)) * up).astype(BF16)
        contrib = jnp.dot(act, wo_ref[lo:hi, :], preferred_element_type=F32)
        part = contrib if part is None else part + contrib
    y = x + 0.5 * part
    if final_norm:
        y = _rms(y, gf_ref[...])
    o_ref[...] = y


def _ffn(x, g, w_in, w_out, layer, half, g_final, *, final_norm=False):
    m, d = x.shape
    d_ff = w_out.shape[2]
    tm = _row_tile(m, 512)
    kern = functools.partial(_ffn_kernel, final_norm=final_norm, sub=min(d_ff, FFN_SUB_COLS))
    resident = dict(pipeline_mode=pl.Buffered(1))
    return pl.pallas_call(
        kern,
        grid=(m // tm,),
        in_specs=[
            pl.BlockSpec((tm, d), lambda i: (i, 0)),
            pl.BlockSpec((1, d), lambda i: (0, 0)),
            pl.BlockSpec((None, None, d, d_ff), lambda i: (layer, half, 0, 0), **resident),
            pl.BlockSpec((None, None, d, d_ff), lambda i: (layer, half, 0, 1), **resident),
            pl.BlockSpec((None, None, d_ff, d), lambda i: (layer, half, 0, 0), **resident),
            pl.BlockSpec((1, d), lambda i: (0, 0)),
        ],
        out_specs=pl.BlockSpec((tm, d), lambda i: (i, 0)),
        out_shape=jax.ShapeDtypeStruct((m, d), F32),
        compiler_params=_cparams("parallel"),
        name="ffn",
    )(x, g.reshape(1, d), w_in, w_in, w_out, g_final.reshape(1, d))


def _linear_kernel(*refs, has_res):
    if has_res:
        a_ref, w_ref, r_ref, o_ref = refs
    else:
        a_ref, w_ref, o_ref = refs
    acc = jnp.dot(a_ref[...], w_ref[...], preferred_element_type=F32)
    if has_res:
        acc = acc + r_ref[...]
    o_ref[...] = acc.astype(o_ref.dtype)


def _linear(a, w, residual=None, out_dtype=F32):
    m, k = a.shape
    n = w.shape[1]
    tm = _row_tile(m, 512)
    tn = _row_tile(n, 1024)
    in_specs = [pl.BlockSpec((tm, k), lambda i, j: (i, 0)), pl.BlockSpec((k, tn), lambda i, j: (0, j))]
    args = [a, w]
    if residual is not None:
        in_specs.append(pl.BlockSpec((tm, tn), lambda i, j: (i, j)))
        args.append(residual)
    return pl.pallas_call(
        functools.partial(_linear_kernel, has_res=residual is not None),
        grid=(m // tm, n // tn),
        in_specs=in_specs,
        out_specs=pl.BlockSpec((tm, tn), lambda i, j: (i, j)),
        out_shape=jax.ShapeDtypeStruct((m, n), out_dtype),
        compiler_params=_cparams("parallel", "parallel"),
        name="linear",
    )(*args)


def _mla_proj_kernel(x_ref, g1_ref, win_ref, gq_ref, gkv_ref, wqb_ref, wukv_ref,
                     cq_ref, sq_ref, ck_ref, sk_ref, vone_ref,
                     q_ref, ckv_ref, kr_ref, *kv_refs, q_lora, kv_lora, with_kv):
    h = _rms(x_ref[...], g1_ref[...]).astype(BF16)
    comb = jnp.dot(h, win_ref[...], preferred_element_type=F32)
    cq = _rms(comb[:, :q_lora], gq_ref[...])
    ckv = _rms(comb[:, q_lora:q_lora + kv_lora], gkv_ref[...])
    blk = comb[:, q_lora + kv_lora:]
    kr = blk * ck_ref[...] + pltpu.roll(blk, HEAD_BLOCK - QK_ROPE, 1) * sk_ref[...]
    ckv_ref[...] = ckv
    kr_ref[...] = kr

    q = jnp.dot(cq.astype(BF16), wqb_ref[...], preferred_element_type=F32)
    cq_t = cq_ref[...]
    sq_t = sq_ref[...]
    for hd in range(MLA_HEADS):
        sl = slice(hd * HEAD_BLOCK, (hd + 1) * HEAD_BLOCK)
        qb = q[:, sl]
        q_ref[:, sl] = (qb * cq_t + pltpu.roll(qb, HEAD_BLOCK - QK_ROPE, 1) * sq_t).astype(BF16)

    if with_kv:
        k_ref, v_ref = kv_refs
        kv = jnp.dot(ckv.astype(BF16), wukv_ref[...], preferred_element_type=F32)
        width = MLA_HEADS * HEAD_BLOCK
        for hd in range(MLA_HEADS):
            sl = slice(hd * HEAD_BLOCK, (hd + 1) * HEAD_BLOCK)
            k_ref[:, sl] = (kv[:, sl] + kr).astype(BF16)
        v_ref[...] = (kv[:, width:] + vone_ref[...]).astype(BF16)


def _mla_proj(x, g1, w_in_ext, g_q, g_kv, w_qb_ext, w_ukv_ext, tabs, v_one, with_kv):
    m, d = x.shape
    q_lora = g_q.shape[0]
    kv_lora = g_kv.shape[0]
    tm = _row_tile(m, 512)
    width = MLA_HEADS * HEAD_BLOCK
    n_pos = tabs[0].shape[0] // tm
    full = lambda a: pl.BlockSpec(a.shape, lambda i: (0,) * a.ndim)
    tab_spec = pl.BlockSpec((tm, LANES), lambda i: (i % n_pos, 0))
    row = lambda n: pl.BlockSpec((tm, n), lambda i: (i, 0))
    out_shape = [jax.ShapeDtypeStruct((m, width), BF16), jax.ShapeDtypeStruct((m, kv_lora), F32),
                 jax.ShapeDtypeStruct((m, LANES), F32)]
    out_specs = [row(width), row(kv_lora), row(LANES)]
    if with_kv:
        out_shape += [jax.ShapeDtypeStruct((m, width), BF16), jax.ShapeDtypeStruct((m, width), BF16)]
        out_specs += [row(width), row(width)]
    g1 = g1.reshape(1, d)
    g_q = g_q.reshape(1, q_lora)
    g_kv = g_kv.reshape(1, kv_lora)
    return pl.pallas_call(
        functools.partial(_mla_proj_kernel, q_lora=q_lora, kv_lora=kv_lora, with_kv=with_kv),
        grid=(m // tm,),
        in_specs=[row(d), full(g1), full(w_in_ext), full(g_q), full(g_kv), full(w_qb_ext), full(w_ukv_ext),
                  tab_spec, tab_spec, tab_spec, tab_spec, full(v_one)],
        out_specs=out_specs,
        out_shape=out_shape,
        compiler_params=_cparams("parallel"),
        name="mla_proj",
    )(x, g1, w_in_ext, g_q, g_kv, w_qb_ext, w_ukv_ext, *tabs, v_one)


def _mla_flash_kernel(q_ref, k_ref, v_ref, o_ref, m_sc, acc_sc, sa_sc, sb_sc, *, tq):
    def q_block(qi, carry):
        _mla_flash_q_block(qi, q_ref, k_ref, v_ref, o_ref, m_sc, acc_sc, sa_sc, sb_sc, tq)
        return carry

    lax.fori_loop(0, q_ref.shape[0] // tq, q_block, 0)


def _mla_flash_q_block(qi, q_ref, k_ref, v_ref, o_ref, m_sc, acc_sc, sa_sc, sb_sc, tq):
    tk = tq
    q_start = pl.multiple_of(qi * tq, tq)
    heads = (slice(0, HEAD_BLOCK), slice(HEAD_BLOCK, 2 * HEAD_BLOCK))
    m_sc[...] = jnp.full(m_sc.shape, NEG, F32)
    acc_sc[...] = jnp.zeros(acc_sc.shape, F32)

    def scores(kc, hh):
        start = pl.multiple_of(kc * tk, tk)
        return lax.dot_general(q_ref[pl.ds(q_start, tq), heads[hh]], k_ref[pl.ds(start, tk), heads[hh]], NT_DIMS,
                               preferred_element_type=F32)

    def softmax_pv(kc, hh, s, masked):
        start = pl.multiple_of(kc * tk, tk)
        if masked:
            row = lax.broadcasted_iota(jnp.int32, (tq, tk), 0)
            col = lax.broadcasted_iota(jnp.int32, (tq, tk), 1)
            s = jnp.where(col <= row, s, NEG)
        m_prev = m_sc[hh]
        m_new = jnp.maximum(m_prev, jnp.max(s, axis=-1, keepdims=True))
        alpha = jnp.exp2(m_prev - m_new)
        p = jnp.exp2(s - jnp.concatenate([m_new] * (tk // LANES), axis=1))
        v = v_ref[pl.ds(start, tk), heads[hh]]
        acc_sc[hh] = acc_sc[hh] * alpha + jnp.dot(p.astype(BF16), v, preferred_element_type=F32)
        m_sc[hh] = m_new

    def chunk(kc, cur, nxt, masked):
        if nxt is not None:
            for hh in range(2):
                nxt[hh] = scores(kc + 1, hh)
        for hh in range(2):
            softmax_pv(kc, hh, cur[hh], masked)

    for hh in range(2):
        sa_sc[hh] = scores(0, hh)

    def body(pair, carry):
        chunk(2 * pair, sa_sc, sb_sc, False)
        chunk(2 * pair + 1, sb_sc, sa_sc, False)
        return carry

    lax.fori_loop(0, lax.shift_right_logical(qi, 1), body, 0)
    odd = (qi & 1) == 1

    @pl.when(odd)
    def _():
        chunk(qi - 1, sa_sc, sb_sc, False)
        chunk(qi, sb_sc, None, True)

    @pl.when(jnp.logical_not(odd))
    def _():
        chunk(qi, sa_sc, None, True)

    lane = lax.broadcasted_iota(jnp.int32, (tq, HEAD_BLOCK), 1)
    acc0 = acc_sc[0]
    acc1 = acc_sc[1]
    out0 = acc0 / acc0[:, V_DIM:V_DIM + 1]
    out1 = acc1 / acc1[:, 0:1]
    o_ref[pl.ds(q_start, tq), :] = jnp.where(lane < V_DIM, out0, out1).astype(o_ref.dtype)


def _mla_flash(q, k, v, batch, seq):
    width = MLA_HEADS * HEAD_BLOCK
    q = q.reshape(batch, seq, width)
    k = k.reshape(batch, seq, width)
    v = v.reshape(batch, seq, width)
    tq = _row_tile(seq, 512)
    pair = 2 * HEAD_BLOCK
    out = pl.pallas_call(
        functools.partial(_mla_flash_kernel, tq=tq),
        grid=(batch, MLA_HEADS // 2),
        in_specs=[pl.BlockSpec((None, seq, pair), lambda b, h: (b, 0, h))] * 3,
        out_specs=pl.BlockSpec((None, seq, 2 * V_DIM), lambda b, h: (b, 0, h)),
        out_shape=jax.ShapeDtypeStruct((batch, seq, MLA_HEADS * V_DIM), BF16),
        scratch_shapes=[pltpu.VMEM((2, tq, HEAD_BLOCK), F32), pltpu.VMEM((2, tq, HEAD_BLOCK), F32),
                        pltpu.VMEM((2, tq, tq), F32), pltpu.VMEM((2, tq, tq), F32)],
        compiler_params=_cparams("parallel", "parallel"),
        name="mla_flash",
    )(q, k, v)
    return out.reshape(batch * seq, MLA_HEADS * V_DIM)


def _mla_decode_kernel(pt_ref, ql_ref, qr_ref, newt_ref, *rest, chain_tiles, page, kv_lora, t_new):
    n_pages = sum(chain_tiles) - 1
    page_refs = rest[:n_pages]
    o_ref = rest[n_pages]
    scratch = rest[n_pages + 1:]
    rows = ql_ref.shape[0]
    tiles = list(page_refs) + [newt_ref]
    ql = ql_ref[...]
    qr = qr_ref[...]
    stats = []
    first = 0
    for g, n_tiles in enumerate(chain_tiles):
        lat_sc, kr_sc = scratch[2 * g], scratch[2 * g + 1]
        for i, ref in enumerate(tiles[first:first + n_tiles]):
            lat_sc[:, i * page:(i + 1) * page] = ref[:kv_lora, :].astype(BF16)
            kr_sc[:, i * page:(i + 1) * page] = ref[kv_lora:, :].astype(BF16)
        first += n_tiles
        s = (jnp.dot(ql, lat_sc[...], preferred_element_type=F32)
             + jnp.dot(qr, kr_sc[...], preferred_element_type=F32))
        if g == len(chain_tiles) - 1:
            n_keys = n_tiles * page
            col = lax.broadcasted_iota(jnp.int32, (rows, n_keys), 1)
            t_row = lax.shift_right_logical(lax.broadcasted_iota(jnp.int32, (rows, n_keys), 0),
                                            int(math.log2(MLA_HEADS)))
            j_new = col - (n_tiles - 1) * page
            s = jnp.where((j_new < 0) | ((j_new <= t_row) & (j_new < t_new)), s, NEG)
        m = jnp.max(s, axis=-1, keepdims=True)
        p = jnp.exp(s - m)
        l = jnp.sum(p, axis=-1, keepdims=True)
        pv = lax.dot_general(p.astype(BF16), lat_sc[...], NT_DIMS, preferred_element_type=F32)
        stats.append((m, l, pv))
    m_all = functools.reduce(jnp.maximum, [st[0] for st in stats])
    num = den = None
    for m, l, pv in stats:
        a = jnp.exp(m - m_all)
        num = a * pv if num is None else num + a * pv
        den = a * l if den is None else den + a * l
    o_ref[...] = (num / den).astype(o_ref.dtype)


def _mla_decode(q_lat, q_rope, new_t, cache_t, layer, page_table):
    bd, rows, kv_lora = q_lat.shape
    n_pages = page_table.shape[1]
    row_w, page = cache_t.shape[2], cache_t.shape[3]
    t_new = rows // MLA_HEADS
    n_chains = 4 if n_pages >= 8 else 1
    base, extra = divmod(n_pages + 1, n_chains)
    chain_tiles = tuple(base + (1 if g >= n_chains - extra else 0) for g in range(n_chains))

    def page_spec(i):
        return pl.BlockSpec((None, None, row_w, page), lambda b, pt: (layer, pt[b, i], 0, 0))

    scratch = []
    for n_tiles in chain_tiles:
        scratch += [pltpu.VMEM((kv_lora, n_tiles * page), BF16), pltpu.VMEM((QK_ROPE, n_tiles * page), BF16)]
    grid_spec = pltpu.PrefetchScalarGridSpec(
        num_scalar_prefetch=1,
        grid=(bd,),
        in_specs=[
            pl.BlockSpec((None, rows, kv_lora), lambda b, pt: (b, 0, 0)),
            pl.BlockSpec((None, rows, QK_ROPE), lambda b, pt: (b, 0, 0)),
            pl.BlockSpec((None, row_w, page), lambda b, pt: (b, 0, 0)),
        ] + [page_spec(i) for i in range(n_pages)],
        out_specs=pl.BlockSpec((None, rows, kv_lora), lambda b, pt: (b, 0, 0)),
        scratch_shapes=scratch,
    )
    kern = functools.partial(_mla_decode_kernel, chain_tiles=chain_tiles, page=page, kv_lora=kv_lora, t_new=t_new)
    return pl.pallas_call(
        kern,
        grid_spec=grid_spec,
        out_shape=jax.ShapeDtypeStruct((bd, rows, kv_lora), BF16),
        compiler_params=_cparams("parallel"),
        name="mla_decode",
    )(page_table, q_lat, q_rope, new_t, *([cache_t] * n_pages))


def _norm_rope_kernel(x_ref, g_ref, w_ref, c_ref, s1_ref, s2_ref, *refs, rope_blocks, group_dils, group_blocks,
                      n_f32_out):
    n_groups = 0 if group_dils is None else len(group_dils)
    n_out = (n_groups if group_dils is not None else 0) + n_f32_out
    out_refs = refs[:n_out]
    sc = refs[n_out] if group_dils is not None else None
    tm = x_ref.shape[0]
    h = _rms(x_ref[...], g_ref[...]).astype(BF16)
    y = jnp.dot(h, w_ref[...], preferred_element_type=F32)
    ct, s1, s2 = c_ref[...], s1_ref[...], s2_ref[...]
    half = ROT_DIM // 2
    for j, (roped, dest) in enumerate(rope_blocks):
        blk = y[:, j * LANES:(j + 1) * LANES]
        if roped:
            blk = blk * ct + pltpu.roll(blk, LANES - half, 1) * s1 + pltpu.roll(blk, half, 1) * s2
        if n_f32_out:
            out_refs[n_out - 1][:, dest * LANES:(dest + 1) * LANES] = blk
        if group_dils is not None:
            gi, jj = divmod(dest, group_blocks)
            dil = group_dils[gi]
            width = group_blocks * LANES
            if dil == 1:
                out_refs[gi][:, jj * LANES:(jj + 1) * LANES] = blk.astype(BF16)
            else:
                sc[...] = blk
                for r in range(dil):
                    lo = r * width + jj * LANES
                    out_refs[gi][:, lo:lo + LANES] = sc[pl.ds(r, tm // dil, stride=dil), :].astype(BF16)


def _norm_rope(x, g, w, tabs, rope_blocks, *, group_dils=None, group_blocks=None, emit_f32=False):
    m, d = x.shape
    n = w.shape[1]
    tm = _row_tile(m, 512)
    n_pos = tabs[0].shape[0] // tm
    tab_spec = pl.BlockSpec((tm, LANES), lambda i: (i % n_pos, 0))
    out_shape, out_specs, scratch = [], [], []
    if group_dils is not None:
        width = group_blocks * LANES
        for dil in group_dils:
            assert tm % (dil * 2 * SUBLANES) == 0
            out_shape.append(jax.ShapeDtypeStruct((m // dil, dil * width), BF16))
            out_specs.append(pl.BlockSpec((tm // dil, dil * width), lambda i: (i, 0)))
        scratch = [pltpu.VMEM((tm, LANES), F32)]
    else:
        emit_f32 = True
    if emit_f32:
        out_shape.append(jax.ShapeDtypeStruct((m, n), F32))
        out_specs.append(pl.BlockSpec((tm, n), lambda i: (i, 0)))
    return pl.pallas_call(
        functools.partial(_norm_rope_kernel, rope_blocks=tuple(rope_blocks), group_dils=group_dils,
                          group_blocks=group_blocks, n_f32_out=int(emit_f32)),
        grid=(m // tm,),
        in_specs=[pl.BlockSpec((tm, d), lambda i: (i, 0)), pl.BlockSpec((1, d), lambda i: (0, 0)),
                  pl.BlockSpec((d, n), lambda i: (0, 0)), tab_spec, tab_spec, tab_spec],
        out_specs=out_specs,
        out_shape=out_shape,
        scratch_shapes=scratch,
        compiler_params=_cparams("parallel"),
        name="norm_rope",
    )(x, g.reshape(1, d), w, *tabs)


def _dil_prompt_kernel(q_ref, kp_ref, kc_ref, vp_ref, vc_ref, o_ref, lse_ref, *, tq, n_sub):
    cblk = pl.program_id(2)
    lane = lax.broadcasted_iota(jnp.int32, (tq, LANES), 1)
    i_idx = lax.broadcasted_iota(jnp.int32, (2 * tq, 2 * tq), 0) & (tq - 1)
    j_idx = lax.broadcasted_iota(jnp.int32, (2 * tq, 2 * tq), 1)
    valid_cur = (j_idx >= tq) & (j_idx - tq <= i_idx)
    valid_mid = ((j_idx < tq) & (j_idx >= i_idx)) | valid_cur
    prev_off = jnp.where(cblk > 0, 0, tq)
    valid_first = ((j_idx < tq) & (j_idx >= i_idx + prev_off)) | valid_cur
    for sub in range(n_sub):
        rs = slice(sub * tq, (sub + 1) * tq)
        ps = slice((sub - 1) * tq, sub * tq)
        valid = valid_first if sub == 0 else valid_mid
        for hp in range(HEADS_PER_GROUP // 2):
            sl = slice(hp * LANES, (hp + 1) * LANES)
            q2 = q_ref[rs, sl]
            zero = jnp.zeros_like(q2)
            qs = jnp.concatenate([jnp.where(lane < HEAD_DIM, q2, zero), jnp.where(lane >= HEAD_DIM, q2, zero)],
                                 axis=0)
            k_prev = kp_ref[:, sl] if sub == 0 else kc_ref[ps, sl]
            v_prev = vp_ref[:, sl] if sub == 0 else vc_ref[ps, sl]
            kcat = jnp.concatenate([k_prev, kc_ref[rs, sl]], axis=0)
            vcat = jnp.concatenate([v_prev, vc_ref[rs, sl]], axis=0)
            s = jnp.where(valid, lax.dot_general(qs, kcat, NT_DIMS, preferred_element_type=F32), NEG)
            m = jnp.max(s, axis=-1, keepdims=True)
            p = jnp.exp(s - m)
            den = jnp.sum(p, axis=-1, keepdims=True)
            o2 = jnp.dot(p.astype(BF16), vcat, preferred_element_type=F32) / den
            lse2 = jnp.broadcast_to(m + jnp.log(den), (2 * tq, LANES))
            o_ref[rs, sl] = jnp.where(lane < HEAD_DIM, o2[:tq], o2[tq:])
            lse_ref[rs, sl] = jnp.where(lane < HEAD_DIM, lse2[:tq], lse2[tq:])


def _dil_prompt_group(q_g, kv_g, dil, batch, seq):
    assert seq % (dil * DIL_N) == 0
    rows = seq // dil
    tq = DIL_N
    n_sub = min(4, rows // tq)
    nblk = rows // (tq * n_sub)
    qv = q_g.reshape(batch, rows, dil * GROUP_WIDTH)
    kvv = kv_g.reshape(batch, rows, dil * 2 * GROUP_WIDTH)
    blk = (None, tq * n_sub, GROUP_WIDTH)
    pblk = (None, tq, GROUP_WIDTH)
    cur = lambda off: pl.BlockSpec(blk, lambda b, r, c: (b, c, 2 * r + off))
    prev = lambda off: pl.BlockSpec(pblk, lambda b, r, c: (b, jnp.maximum(c * n_sub - 1, 0), 2 * r + off))
    rspec = pl.BlockSpec(blk, lambda b, r, c: (b, c, r))
    o, lse = pl.pallas_call(
        functools.partial(_dil_prompt_kernel, tq=tq, n_sub=n_sub),
        grid=(batch, dil, nblk),
        in_specs=[rspec, prev(0), cur(0), prev(1), cur(1)],
        out_specs=[rspec, rspec],
        out_shape=[jax.ShapeDtypeStruct((batch, rows, dil * GROUP_WIDTH), F32)] * 2,
        compiler_params=_cparams("parallel", "parallel", "arbitrary"),
        name="dil_prompt",
    )(qv, kvv, kvv, kvv, kvv)
    return o.reshape(batch * rows, dil * GROUP_WIDTH), lse.reshape(batch * rows, dil * GROUP_WIDTH)


def _dil_sample_kernel(q_ref, buf_ref, new_ref, *rest, bb, dil, t_new, emit_state):
    if emit_state:
        o_ref, lse_ref, state_ref = rest
    else:
        o_ref, lse_ref = rest
    w = buf_ref.shape[2]
    rows = t_new * HEADS_PER_GROUP
    log_h = int(math.log2(HEADS_PER_GROUP))
    lane = lax.broadcasted_iota(jnp.int32, (rows, GROUP_WIDTH), 1)
    row = lax.broadcasted_iota(jnp.int32, (rows, GROUP_WIDTH), 0)
    diag = lax.shift_right_logical(lane, int(math.log2(HEAD_DIM))) == (row & (HEADS_PER_GROUP - 1))
    w_idx = lax.broadcasted_iota(jnp.int32, (rows, w), 1)
    t_of_row = lax.shift_right_logical(lax.broadcasted_iota(jnp.int32, (rows, w), 0), log_h)
    valid = (w_idx >= t_of_row) if dil == 1 else ((w_idx & (dil - 1)) == t_of_row)
    n_new = new_ref.shape[1]
    j_new = lax.broadcasted_iota(jnp.int32, (rows, n_new), 1)
    t_new_row = lax.shift_right_logical(lax.broadcasted_iota(jnp.int32, (rows, n_new), 0), log_h)
    valid_new = (j_new <= t_new_row) if dil == 1 else (j_new == t_new_row)
    shift_lane = lax.broadcasted_iota(jnp.int32, (2 * GROUP_WIDTH, LANES), 1)
    for b in range(bb):
        q = q_ref[b]
        qrep = jnp.concatenate(
            [jnp.broadcast_to(q[t:t + 1, :], (HEADS_PER_GROUP, GROUP_WIDTH)) for t in range(t_new)], axis=0)
        qbd = jnp.where(diag, qrep, 0.0).astype(BF16)
        kt = buf_ref[b, :GROUP_WIDTH, :].astype(BF16)
        vt = buf_ref[b, GROUP_WIDTH:, :].astype(BF16)
        k_new = new_ref[b, :, :GROUP_WIDTH].astype(BF16)
        v_new = new_ref[b, :, GROUP_WIDTH:].astype(BF16)
        s = jnp.where(valid, jnp.dot(qbd, kt, preferred_element_type=F32), NEG)
        s_new = jnp.where(valid_new, lax.dot_general(qbd, k_new, NT_DIMS, preferred_element_type=F32), NEG)
        m = jnp.maximum(jnp.max(s, axis=-1, keepdims=True), jnp.max(s_new, axis=-1, keepdims=True))
        p = jnp.exp(s - m)
        p_new = jnp.exp(s_new - m)
        den = jnp.sum(p, axis=-1, keepdims=True) + jnp.sum(p_new, axis=-1, keepdims=True)
        o = (lax.dot_general(p.astype(BF16), vt, NT_DIMS, preferred_element_type=F32)
             + jnp.dot(p_new.astype(BF16), v_new, preferred_element_type=F32)) / den
        lse = jnp.broadcast_to(m + jnp.log(den), (rows, GROUP_WIDTH))
        o_ref[b] = jnp.sum(jnp.where(diag, o, 0.0).reshape(t_new, HEADS_PER_GROUP, GROUP_WIDTH), axis=1)
        lse_ref[b] = jnp.sum(jnp.where(diag, lse, 0.0).reshape(t_new, HEADS_PER_GROUP, GROUP_WIDTH), axis=1)
        if emit_state:
            n_col = w // LANES
            nxt = pltpu.roll(buf_ref[b, :, 0:LANES], LANES - t_new, 1)
            for jc in range(n_col):
                cur = nxt
                if jc + 1 < n_col:
                    nxt = pltpu.roll(buf_ref[b, :, (jc + 1) * LANES:(jc + 2) * LANES], LANES - t_new, 1)
                    cur = jnp.where(shift_lane < LANES - t_new, cur, nxt)
                state_ref[b, :, jc * LANES:(jc + 1) * LANES] = cur
            new_rows = jnp.concatenate([new_ref[b], jnp.zeros((LANES - n_new, 2 * GROUP_WIDTH), F32)], axis=0)
            state_ref[b, :, w - t_new:w] = new_rows.T[:, :t_new]


def _dil_sample_group(q_g, buf_t, new_g, dil, emit_state):
    bd, t_new, _ = q_g.shape
    w = buf_t.shape[2]
    assert w == dil * DIL_N, "state buffer must hold exactly one window"
    assert dil == 1 or dil >= t_new
    kv_w = 2 * GROUP_WIDTH
    bb = _row_tile(bd, max(1, 1024 // w))
    spec3 = lambda a: pl.BlockSpec((bb,) + a.shape[1:], lambda i: (i, 0, 0))
    in_specs = [spec3(q_g), spec3(buf_t), spec3(new_g)]
    args = [q_g, buf_t, new_g]
    out_spec = pl.BlockSpec((bb, t_new, GROUP_WIDTH), lambda i: (i, 0, 0))
    out_specs = [out_spec, out_spec]
    out_shape = [jax.ShapeDtypeStruct((bd, t_new, GROUP_WIDTH), F32)] * 2
    if emit_state:
        out_specs.append(spec3(buf_t))
        out_shape.append(jax.ShapeDtypeStruct(buf_t.shape, F32))
    res = pl.pallas_call(
        functools.partial(_dil_sample_kernel, bb=bb, dil=dil, t_new=t_new, emit_state=emit_state),
        grid=(bd // bb,),
        in_specs=in_specs,
        out_specs=out_specs,
        out_shape=out_shape,
        compiler_params=_cparams("parallel"),
        name="dil_sample",
    )(*args)
    o, lse = res[0].reshape(bd * t_new, GROUP_WIDTH), res[1].reshape(bd * t_new, GROUP_WIDTH)
    return o, lse, (res[2] if emit_state else None)


def _dil_out_kernel(*refs, group_dils):
    n_g = N_GROUPS
    o_refs, l_refs = refs[:n_g], refs[n_g:2 * n_g]
    w_ref, x_ref, y_ref = refs[2 * n_g:2 * n_g + 3]
    scratch = refs[2 * n_g + 3:]
    tm = x_ref.shape[0]
    cols = []
    for jj in range(GROUP_LANE_BLOCKS):
        sl = slice(jj * LANES, (jj + 1) * LANES)
        os_, ls_ = [], []
        for gi in range(n_g):
            dil = 1 if group_dils is None else group_dils[gi]
            if dil == 1:
                os_.append(o_refs[gi][:, sl])
                ls_.append(l_refs[gi][:, sl])
            else:
                for src, dst in ((o_refs[gi], scratch[0]), (l_refs[gi], scratch[1])):
                    for r in range(dil):
                        lo = r * GROUP_WIDTH + jj * LANES
                        dst[pl.ds(r, tm // dil, stride=dil), :] = src[:, lo:lo + LANES]
                os_.append(scratch[0][...])
                ls_.append(scratch[1][...])
        m = jnp.maximum(jnp.maximum(ls_[0], ls_[1]), ls_[2])
        e = [jnp.exp(l - m) for l in ls_]
        den = e[0] + e[1] + e[2]
        cols.append(((e[0] / den) * os_[0] + (e[1] / den) * os_[1] + (e[2] / den) * os_[2]).astype(BF16))
    o = jnp.concatenate(cols, axis=1)
    y_ref[...] = x_ref[...] + jnp.dot(o, w_ref[...], preferred_element_type=F32)


def _dil_out(outs, lses, w_o, x, group_dils=None):
    m, d = x.shape
    tm = _row_tile(m, 512)
    specs = []
    for gi in range(N_GROUPS):
        dil = 1 if group_dils is None else group_dils[gi]
        specs.append(pl.BlockSpec((tm // dil, dil * GROUP_WIDTH), lambda i: (i, 0)))
    scratch = [] if group_dils is None else [pltpu.VMEM((tm, LANES), F32)] * 2
    return pl.pallas_call(
        functools.partial(_dil_out_kernel, group_dils=group_dils),
        grid=(m // tm,),
        in_specs=specs + specs + [pl.BlockSpec(w_o.shape, lambda i: (0, 0)), pl.BlockSpec((tm, d), lambda i: (i, 0))],
        out_specs=pl.BlockSpec((tm, d), lambda i: (i, 0)),
        out_shape=jax.ShapeDtypeStruct((m, d), F32),
        scratch_shapes=scratch,
        compiler_params=_cparams("parallel"),
        name="dil_out",
    )(*outs, *lses, w_o, x)


def _rope_cos_sin(pos, rot_dim, theta):
    half = rot_dim // 2
    inv = jnp.float32(theta) ** (-2.0 * jnp.arange(half, dtype=F32) / rot_dim)
    ang = pos[:, None] * inv[None, :]
    return jnp.cos(ang), jnp.sin(ang)


def _mla_tables(pos, q_scale):
    cos, sin = _rope_cos_sin(pos, QK_ROPE, MLA_THETA)
    n = pos.shape[0]
    z = lambda w: jnp.zeros((n, w), F32)
    cos2 = jnp.concatenate([cos, cos], axis=1)
    sin2 = jnp.concatenate([sin, sin], axis=1)
    rest = HEAD_BLOCK - QK_NOPE - QK_ROPE
    cq = jnp.concatenate([jnp.ones((n, QK_NOPE), F32), cos2, z(rest)], axis=1) * q_scale
    sq = jnp.concatenate([z(QK_NOPE), sin2, z(rest)], axis=1) * q_scale
    ck = jnp.concatenate([z(QK_NOPE), cos2, z(rest)], axis=1)
    sk = jnp.concatenate([z(QK_NOPE), sin2, z(rest)], axis=1)
    return cq, sq, ck, sk


def _dil_tables(pos, scale):
    cos, sin = _rope_cos_sin(pos, ROT_DIM, ROPE_THETA)
    n = pos.shape[0]
    half = ROT_DIM // 2
    z = lambda w: jnp.zeros((n, w), F32)
    keep = jnp.ones((n, HEAD_DIM - ROT_DIM), F32)
    c_head = jnp.concatenate([cos, cos, keep], axis=1)
    s1_head = jnp.concatenate([-sin, z(HEAD_DIM - half)], axis=1)
    s2_head = jnp.concatenate([z(half), sin, z(HEAD_DIM - ROT_DIM)], axis=1)
    two = lambda a: jnp.concatenate([a, a], axis=1) * scale
    return two(c_head), two(s1_head), two(s2_head)


def _rot_half_cols(w):
    half = w.shape[-1] // 2
    return jnp.concatenate([-w[..., half:], w[..., :half]], axis=-1)


def _prep_mla_weights(w_in, w_qb, w_ukv, q_lora, kv_lora):
    d = w_in.shape[0]
    w_kr = w_in[:, q_lora + kv_lora:]
    assert HEAD_BLOCK - QK_NOPE - QK_ROPE == QK_ROPE
    w_in_ext = jnp.concatenate(
        [w_in[:, :q_lora + kv_lora], jnp.zeros((d, QK_NOPE), F32), w_kr, _rot_half_cols(w_kr)], axis=1)
    wq = w_qb.reshape(q_lora, MLA_HEADS, QK_NOPE + QK_ROPE)
    wq_rope = wq[..., QK_NOPE:]
    w_qb_ext = jnp.concatenate([wq[..., :QK_NOPE], wq_rope, _rot_half_cols(wq_rope)], axis=-1)
    w_qb_ext = w_qb_ext.reshape(q_lora, MLA_HEADS * HEAD_BLOCK)
    wkv = w_ukv.reshape(kv_lora, MLA_HEADS, QK_NOPE + V_DIM)
    zpad = jnp.zeros((kv_lora, MLA_HEADS, HEAD_BLOCK - QK_NOPE), F32)
    wk = jnp.concatenate([wkv[..., :QK_NOPE], zpad], axis=-1).reshape(kv_lora, MLA_HEADS * HEAD_BLOCK)
    wv_even = jnp.concatenate([wkv[..., QK_NOPE:], zpad], axis=-1)
    wv_odd = jnp.concatenate([zpad, wkv[..., QK_NOPE:]], axis=-1)
    odd = (jnp.arange(MLA_HEADS) % 2 == 1)[None, :, None]
    wv = jnp.where(odd, wv_odd, wv_even).reshape(kv_lora, MLA_HEADS * HEAD_BLOCK)
    w_ukv_ext = jnp.concatenate([wk, wv], axis=1)
    one = np.zeros((1, MLA_HEADS, HEAD_BLOCK), np.float32)
    one[0, 0::2, V_DIM] = 1.0
    one[0, 1::2, 0] = 1.0
    v_one = jnp.asarray(one.reshape(1, MLA_HEADS * HEAD_BLOCK))
    eye = jnp.eye(MLA_HEADS, dtype=F32)
    w_uk = wkv[..., :QK_NOPE]
    w_uk_rows = jnp.concatenate([jnp.transpose(w_uk, (1, 2, 0)),
                                 jnp.zeros((MLA_HEADS, HEAD_BLOCK - QK_NOPE, kv_lora), F32)], axis=1)
    w_uk_bd = jnp.einsum('hnc,hg->hngc', w_uk_rows, eye).reshape(MLA_HEADS * HEAD_BLOCK, MLA_HEADS * kv_lora)
    w_uv = jnp.transpose(wkv[..., QK_NOPE:], (1, 0, 2))
    w_uv_bd = jnp.einsum('hcv,hg->hcgv', w_uv, eye).reshape(MLA_HEADS * kv_lora, MLA_HEADS * V_DIM)
    bf = lambda a: a.astype(BF16)
    return bf(w_in_ext), bf(w_qb_ext), bf(w_ukv_ext), v_one, bf(w_uk_bd), bf(w_uv_bd)


def kernel(x_prompt, x_sample, cache_mla, page_table, state_dil_w128, state_dil_w512, state_dil_w2048,
           g_layers, w_ffn_in, w_ffn_out, w_mla_in, g_mla_q, g_mla_kv, w_mla_qb, w_mla_ukv, w_mla_o,
           g_shared_kv, w_shared_kv, w_dil_q, w_dil_o, g_final):
    batch, seq, d_model = x_prompt.shape
    bd, t_new, _ = x_sample.shape
    depth = g_layers.shape[0]
    n_a = w_mla_in.shape[0]
    q_lora = g_mla_q.shape[1]
    kv_lora = g_mla_kv.shape[1]
    page = cache_mla.shape[2]
    past_len = page_table.shape[1] * page
    buffers = [state_dil_w128, state_dil_w512, state_dil_w2048]
    dils = tuple(dil for _, dil in DIL_GROUPS)
    kv_w = 2 * GROUP_WIDTH

    w_ffn_in_b = w_ffn_in.astype(BF16)
    w_ffn_out_b = w_ffn_out.astype(BF16)
    mla_w = [_prep_mla_weights(w_mla_in[a], w_mla_qb[a], w_mla_ukv[a], q_lora, kv_lora) for a in range(n_a)]
    w_mla_o_b = w_mla_o.astype(BF16)
    w_shared_b = w_shared_kv.astype(BF16)
    w_dil_q_b = w_dil_q.astype(BF16)
    w_dil_o_b = w_dil_o.astype(BF16)
    kv_rope_blocks = []
    for is_v in range(2):
        for gi in range(N_GROUPS):
            for jj in range(GROUP_LANE_BLOCKS):
                kv_rope_blocks.append((not is_v, (2 * gi + is_v) * GROUP_LANE_BLOCKS + jj))
    q_rope_blocks = [(True, j) for j in range(DIL_WIDTH // LANES)]

    pos_p = jnp.arange(seq, dtype=F32)
    pos_s = jnp.tile((past_len + jnp.arange(t_new)).astype(F32), bd)

    def ffn(x, l, half, final_norm=False):
        return _ffn(x, g_layers[l, 2 * half], w_ffn_in_b, w_ffn_out_b, l, half, g_final, final_norm=final_norm)

    mla_tabs_p = _mla_tables(pos_p, MLA_SCALE * LOG2E)
    dil_q_tabs_p = _dil_tables(pos_p, DIL_SCALE)
    dil_k_tabs_p = _dil_tables(pos_p, 1.0)
    x = x_prompt.reshape(batch * seq, d_model)
    rows_p = []
    kv_groups = kv_f32_p = None
    for l in range(depth):
        x = ffn(x, l, 0)
        if l < n_a:
            w_in_ext, w_qb_ext, w_ukv_ext, v_one, _, _ = mla_w[l]
            q, ckv, kr, k, v = _mla_proj(x, g_layers[l, 1], w_in_ext, g_mla_q[l], g_mla_kv[l], w_qb_ext, w_ukv_ext,
                                         mla_tabs_p, v_one, with_kv=True)
            o = _mla_flash(q, k, v, batch, seq)
            x = _linear(o, w_mla_o_b[l], residual=x)
            rows_p.append(jnp.concatenate([ckv, kr[:, QK_NOPE:QK_NOPE + QK_ROPE]], axis=1))
        else:
            q_groups = _norm_rope(x, g_layers[l, 1], w_dil_q_b[l - n_a], dil_q_tabs_p, q_rope_blocks,
                                  group_dils=dils, group_blocks=GROUP_LANE_BLOCKS)
            res = [_dil_prompt_group(q_groups[gi], kv_groups[gi], dil, batch, seq) for gi, dil in enumerate(dils)]
            x = _dil_out([r[0] for r in res], [r[1] for r in res], w_dil_o_b[l - n_a], x, group_dils=dils)
        x = ffn(x, l, 1, final_norm=(l == depth - 1))
        if l == n_a - 1:
            *kv_groups, kv_f32_p = _norm_rope(x, g_shared_kv, w_shared_b, dil_k_tabs_p, kv_rope_blocks,
                                              group_dils=dils, group_blocks=2 * GROUP_LANE_BLOCKS, emit_f32=True)
    y_prompt = x.reshape(batch, seq, d_model)
    mla_rows_prompt = jnp.stack(rows_p, axis=0).reshape(n_a, batch, seq, kv_lora + QK_ROPE)
    kv_p3 = kv_f32_p.reshape(batch, seq, N_GROUPS * kv_w)
    dil_p = []
    for gi, (win, _) in enumerate(DIL_GROUPS):
        keep = min(win, seq)
        tail = kv_p3[:, seq - keep:, gi * kv_w:(gi + 1) * kv_w]
        dil_p.append(tail.reshape(batch, keep, 2, HEADS_PER_GROUP, HEAD_DIM))

    mla_tabs_s = _mla_tables(pos_s, MLA_SCALE)
    dil_q_tabs_s = _dil_tables(pos_s, DIL_SCALE)
    dil_k_tabs_s = _dil_tables(pos_s, 1.0)
    cache_t = jnp.swapaxes(cache_mla, 2, 3)
    bufs_t = []
    for buf in buffers:
        length = buf.shape[1]
        assert min(length, length + t_new) == length
        bufs_t.append(jnp.transpose(buf, (0, 2, 3, 4, 1)).reshape(bd, kv_w, length))
    new_rows_pad = SUBLANES
    x = x_sample.reshape(bd * t_new, d_model)
    rows_s = []
    kv_new = None
    states_t = [None] * N_GROUPS
    for l in range(depth):
        x = ffn(x, l, 0)
        if l < n_a:
            w_in_ext, w_qb_ext, w_ukv_ext, v_one, w_uk_bd, w_uv_bd = mla_w[l]
            q, ckv, kr = _mla_proj(x, g_layers[l, 1], w_in_ext, g_mla_q[l], g_mla_kv[l], w_qb_ext, w_ukv_ext,
                                   mla_tabs_s, v_one, with_kv=False)
            rows = jnp.concatenate([ckv, kr[:, QK_NOPE:QK_NOPE + QK_ROPE]], axis=1)
            rows_s.append(rows)
            q_lat = _linear(q, w_uk_bd, out_dtype=BF16).reshape(bd, t_new * MLA_HEADS, kv_lora)
            q_rope = q.reshape(bd, t_new * MLA_HEADS, HEAD_BLOCK)[:, :, QK_NOPE:QK_NOPE + QK_ROPE]
            new_pad = jnp.pad(rows.reshape(bd, t_new, kv_lora + QK_ROPE), ((0, 0), (0, page - t_new), (0, 0)))
            o_lat = _mla_decode(q_lat, q_rope, jnp.swapaxes(new_pad, 1, 2), cache_t, l, page_table)
            o = _linear(o_lat.reshape(bd * t_new, MLA_HEADS * kv_lora), w_uv_bd, out_dtype=BF16)
            x = _linear(o, w_mla_o_b[l], residual=x)
        else:
            q = _norm_rope(x, g_layers[l, 1], w_dil_q_b[l - n_a], dil_q_tabs_s, q_rope_blocks)[0]
            qg = q.reshape(bd, t_new, N_GROUPS, GROUP_WIDTH)
            outs, lses = [], []
            for gi, dil in enumerate(dils):
                emit = l == n_a
                o, lse, st = _dil_sample_group(qg[:, :, gi], bufs_t[gi], kv_new[:, :, gi], dil, emit)
                if emit:
                    states_t[gi] = st
                outs.append(o)
                lses.append(lse)
            x = _dil_out(outs, lses, w_dil_o_b[l - n_a], x)
        x = ffn(x, l, 1, final_norm=(l == depth - 1))
        if l == n_a - 1:
            kv_f32_s = _norm_rope(x, g_shared_kv, w_shared_b, dil_k_tabs_s, kv_rope_blocks)[0]
            kv_s = kv_f32_s.reshape(bd, t_new, N_GROUPS, kv_w)
            kv_new = jnp.pad(kv_s, ((0, 0), (0, new_rows_pad - t_new), (0, 0), (0, 0)))
    y_sample = x.reshape(bd, t_new, d_model)
    mla_rows_sample = jnp.stack(rows_s, axis=0).reshape(n_a, bd, t_new, kv_lora + QK_ROPE)
    dil_s = []
    for gi, buf in enumerate(buffers):
        length = buf.shape[1]
        st = states_t[gi].reshape(bd, 2, HEADS_PER_GROUP, HEAD_DIM, length)
        dil_s.append(jnp.transpose(st, (0, 4, 1, 2, 3)))

    return (y_prompt, y_sample, mla_rows_prompt, mla_rows_sample,
            dil_p[0], dil_p[1], dil_p[2], dil_s[0], dil_s[1], dil_s[2])
```

```python
import functools
import math

import numpy as np
import jax
import jax.numpy as jnp
from jax import lax
from jax.experimental import pallas as pl
from jax.experimental.pallas import tpu as pltpu

F32 = jnp.float32
BF16 = jnp.bfloat16

MLA_HEADS = 16
QK_NOPE = 64
QK_ROPE = 32
V_DIM = 64
MLA_THETA = 10000.0
MLA_SCALE = (QK_NOPE + QK_ROPE) ** -0.5
DIL_GROUPS = ((128, 1), (512, 4), (2048, 16))
N_GROUPS = len(DIL_GROUPS)
HEADS_PER_GROUP = 8
HEAD_DIM = 64
GROUP_WIDTH = HEADS_PER_GROUP * HEAD_DIM
DIL_WIDTH = N_GROUPS * GROUP_WIDTH
ROT_DIM = HEAD_DIM // 4
ROPE_THETA = 500000.0
DIL_SCALE = HEAD_DIM ** -0.5
DIL_N = 128
EPS = 1e-6
NEG = -1e30
LOG2E = math.log2(math.e)

LANES = 128
SUBLANES = 8
HEAD_BLOCK = 128
GROUP_LANE_BLOCKS = GROUP_WIDTH // LANES
MXU_DIM = 256
FFN_SUB_COLS = 2 * MXU_DIM
VMEM_LIMIT_BYTES = 56 * 1024 * 1024
NT_DIMS = (((1,), (1,)), ((), ()))


def _cparams(*sem):
    return pltpu.CompilerParams(dimension_semantics=sem, vmem_limit_bytes=VMEM_LIMIT_BYTES)


def _row_tile(m, pref):
    t = min(m, pref)
    assert m % t == 0, (m, t)
    return t


def _rms(x, g):
    return x * lax.rsqrt(jnp.mean(x * x, axis=-1, keepdims=True) + EPS) * g


def _ffn_kernel(x_ref, g_ref, wg_ref, wu_ref, wo_ref, gf_ref, o_ref, *, final_norm, sub):
    x = x_ref[...]
    h = _rms(x, g_ref[...]).astype(BF16)
    d_ff = wg_ref.shape[1]
    part = None
    for lo in range(0, d_ff, sub):
        hi = min(lo + sub, d_ff)
        gate = jnp.dot(h, wg_ref[:, lo:hi], preferred_element_type=F32)
        up = jnp.dot(h, wu_ref[:, lo:hi], preferred_element_type=F32)
        act = (gate / (1.0 + jnp.exp(-gate)) * up).astype(BF16)
        contrib = jnp.dot(act, wo_ref[lo:hi, :], preferred_element_type=F32)
        part = contrib if part is None else part + contrib
    y = x + 0.5 * part
    if final_norm:
        y = _rms(y, gf_ref[...])
    o_ref[...] = y


def _ffn(x, g, w_in, w_out, layer, half, g_final, *, final_norm=False):
    m, d = x.shape
    d_ff = w_out.shape[2]
    tm = _row_tile(m, 512)
    kern = functools.partial(_ffn_kernel, final_norm=final_norm, sub=min(d_ff, FFN_SUB_COLS))
    resident = dict(pipeline_mode=pl.Buffered(1))
    return pl.pallas_call(
        kern,
        grid=(m // tm,),
        in_specs=[
            pl.BlockSpec((tm, d), lambda i: (i, 0)),
            pl.BlockSpec((1, d), lambda i: (0, 0)),
            pl.BlockSpec((None, None, d, d_ff), lambda i: (layer, half, 0, 0), **resident),
            pl.BlockSpec((None, None, d, d_ff), lambda i: (layer, half, 0, 1), **resident),
            pl.BlockSpec((None, None, d_ff, d), lambda i: (layer, half, 0, 0), **resident),
            pl.BlockSpec((1, d), lambda i: (0, 0)),
        ],
        out_specs=pl.BlockSpec((tm, d), lambda i: (i, 0)),
        out_shape=jax.ShapeDtypeStruct((m, d), F32),
        compiler_params=_cparams("parallel"),
        name="ffn",
    )(x, g.reshape(1, d), w_in, w_in, w_out, g_final.reshape(1, d))


def _linear_kernel(*refs, has_res):
    if has_res:
        a_ref, w_ref, r_ref, o_ref = refs
    else:
        a_ref, w_ref, o_ref = refs
    acc = jnp.dot(a_ref[...], w_ref[...], preferred_element_type=F32)
    if has_res:
        acc = acc + r_ref[...]
    o_ref[...] = acc.astype(o_ref.dtype)


def _linear(a, w, residual=None, out_dtype=F32):
    m, k = a.shape
    n = w.shape[1]
    tm = _row_tile(m, 512)
    tn = _row_tile(n, 1024)
    in_specs = [pl.BlockSpec((tm, k), lambda i, j: (i, 0)), pl.BlockSpec((k, tn), lambda i, j: (0, j))]
    args = [a, w]
    if residual is not None:
        in_specs.append(pl.BlockSpec((tm, tn), lambda i, j: (i, j)))
        args.append(residual)
    return pl.pallas_call(
        functools.partial(_linear_kernel, has_res=residual is not None),
        grid=(m // tm, n // tn),
        in_specs=in_specs,
        out_specs=pl.BlockSpec((tm, tn), lambda i, j: (i, j)),
        out_shape=jax.ShapeDtypeStruct((m, n), out_dtype),
        compiler_params=_cparams("parallel", "parallel"),
        name="linear",
    )(*args)


def _mla_proj_kernel(x_ref, g1_ref, win_ref, gq_ref, gkv_ref, wqb_ref, wukv_ref,
                     cq_ref, sq_ref, ck_ref, sk_ref, vone_ref,
                     q_ref, ckv_ref, kr_ref, *kv_refs, q_lora, kv_lora, with_kv):
    h = _rms(x_ref[...], g1_ref[...]).astype(BF16)
    comb = jnp.dot(h, win_ref[...], preferred_element_type=F32)
    cq = _rms(comb[:, :q_lora], gq_ref[...])
    ckv = _rms(comb[:, q_lora:q_lora + kv_lora], gkv_ref[...])
    blk = comb[:, q_lora + kv_lora:]
    kr = blk * ck_ref[...] + pltpu.roll(blk, HEAD_BLOCK - QK_ROPE, 1) * sk_ref[...]
    ckv_ref[...] = ckv
    kr_ref[...] = kr

    q = jnp.dot(cq.astype(BF16), wqb_ref[...], preferred_element_type=F32)
    cq_t = cq_ref[...]
    sq_t = sq_ref[...]
    for hd in range(MLA_HEADS):
        sl = slice(hd * HEAD_BLOCK, (hd + 1) * HEAD_BLOCK)
        qb = q[:, sl]
        q_ref[:, sl] = (qb * cq_t + pltpu.roll(qb, HEAD_BLOCK - QK_ROPE, 1) * sq_t).astype(BF16)

    if with_kv:
        k_ref, v_ref = kv_refs
        kv = jnp.dot(ckv.astype(BF16), wukv_ref[...], preferred_element_type=F32)
        width = MLA_HEADS * HEAD_BLOCK
        for hd in range(MLA_HEADS):
            sl = slice(hd * HEAD_BLOCK, (hd + 1) * HEAD_BLOCK)
            k_ref[:, sl] = (kv[:, sl] + kr).astype(BF16)
        v_ref[...] = (kv[:, width:] + vone_ref[...]).astype(BF16)


def _mla_proj(x, g1, w_in_ext, g_q, g_kv, w_qb_ext, w_ukv_ext, tabs, v_one, with_kv):
    m, d = x.shape
    q_lora = g_q.shape[0]
    kv_lora = g_kv.shape[0]
    tm = _row_tile(m, 512)
    width = MLA_HEADS * HEAD_BLOCK
    n_pos = tabs[0].shape[0] // tm
    full = lambda a: pl.BlockSpec(a.shape, lambda i: (0,) * a.ndim)
    tab_spec = pl.BlockSpec((tm, LANES), lambda i: (i % n_pos, 0))
    row = lambda n: pl.BlockSpec((tm, n), lambda i: (i, 0))
    out_shape = [jax.ShapeDtypeStruct((m, width), BF16), jax.ShapeDtypeStruct((m, kv_lora), F32),
                 jax.ShapeDtypeStruct((m, LANES), F32)]
    out_specs = [row(width), row(kv_lora), row(LANES)]
    if with_kv:
        out_shape += [jax.ShapeDtypeStruct((m, width), BF16), jax.ShapeDtypeStruct((m, width), BF16)]
        out_specs += [row(width), row(width)]
    g1 = g1.reshape(1, d)
    g_q = g_q.reshape(1, q_lora)
    g_kv = g_kv.reshape(1, kv_lora)
    return pl.pallas_call(
        functools.partial(_mla_proj_kernel, q_lora=q_lora, kv_lora=kv_lora, with_kv=with_kv),
        grid=(m // tm,),
        in_specs=[row(d), full(g1), full(w_in_ext), full(g_q), full(g_kv), full(w_qb_ext), full(w_ukv_ext),
                  tab_spec, tab_spec, tab_spec, tab_spec, full(v_one)],
        out_specs=out_specs,
        out_shape=out_shape,
        compiler_params=_cparams("parallel"),
        name="mla_proj",
    )(x, g1, w_in_ext, g_q, g_kv, w_qb_ext, w_ukv_ext, *tabs, v_one)


def _mla_flash_kernel(q_ref, k_ref, v_ref, o_ref, m_sc, acc_sc, sa_sc, sb_sc, *, tq):
    def q_block(qi, carry):
        _mla_flash_q_block(qi, q_ref, k_ref, v_ref, o_ref, m_sc, acc_sc, sa_sc, sb_sc, tq)
        return carry

    lax.fori_loop(0, q_ref.shape[0] // tq, q_block, 0)


def _mla_flash_q_block(qi, q_ref, k_ref, v_ref, o_ref, m_sc, acc_sc, sa_sc, sb_sc, tq):
    tk = tq
    q_start = pl.multiple_of(qi * tq, tq)
    heads = (slice(0, HEAD_BLOCK), slice(HEAD_BLOCK, 2 * HEAD_BLOCK))
    m_sc[...] = jnp.full(m_sc.shape, NEG, F32)
    acc_sc[...] = jnp.zeros(acc_sc.shape, F32)

    def scores(kc, hh):
        start = pl.multiple_of(kc * tk, tk)
        return lax.dot_general(q_ref[pl.ds(q_start, tq), heads[hh]], k_ref[pl.ds(start, tk), heads[hh]], NT_DIMS,
                               preferred_element_type=F32)

    def softmax_pv(kc, hh, s, masked):
        start = pl.multiple_of(kc * tk, tk)
        if masked:
            row = lax.broadcasted_iota(jnp.int32, (tq, tk), 0)
            col = lax.broadcasted_iota(jnp.int32, (tq, tk), 1)
            s = jnp.where(col <= row, s, NEG)
        m_prev = m_sc[hh]
        m_new = jnp.maximum(m_prev, jnp.max(s, axis=-1, keepdims=True))
        alpha = jnp.exp2(m_prev - m_new)
        p = jnp.exp2(s - jnp.concatenate([m_new] * (tk // LANES), axis=1))
        v = v_ref[pl.ds(start, tk), heads[hh]]
        acc_sc[hh] = acc_sc[hh] * alpha + jnp.dot(p.astype(BF16), v, preferred_element_type=F32)
        m_sc[hh] = m_new

    def chunk(kc, cur, nxt, masked):
        if nxt is not None:
            for hh in range(2):
                nxt[hh] = scores(kc + 1, hh)
        for hh in range(2):
            softmax_pv(kc, hh, cur[hh], masked)

    for hh in range(2):
        sa_sc[hh] = scores(0, hh)

    def body(pair, carry):
        chunk(2 * pair, sa_sc, sb_sc, False)
        chunk(2 * pair + 1, sb_sc, sa_sc, False)
        return carry

    lax.fori_loop(0, lax.shift_right_logical(qi, 1), body, 0)
    odd = (qi & 1) == 1

    @pl.when(odd)
    def _():
        chunk(qi - 1, sa_sc, sb_sc, False)
        chunk(qi, sb_sc, None, True)

    @pl.when(jnp.logical_not(odd))
    def _():
        chunk(qi, sa_sc, None, True)

    lane = lax.broadcasted_iota(jnp.int32, (tq, HEAD_BLOCK), 1)
    acc0 = acc_sc[0]
    acc1 = acc_sc[1]
    out0 = acc0 / acc0[:, V_DIM:V_DIM + 1]
    out1 = acc1 / acc1[:, 0:1]
    o_ref[pl.ds(q_start, tq), :] = jnp.where(lane < V_DIM, out0, out1).astype(o_ref.dtype)


def _mla_flash(q, k, v, batch, seq):
    width = MLA_HEADS * HEAD_BLOCK
    q = q.reshape(batch, seq, width)
    k = k.reshape(batch, seq, width)
    v = v.reshape(batch, seq, width)
    tq = _row_tile(seq, 512)
    pair = 2 * HEAD_BLOCK
    out = pl.pallas_call(
        functools.partial(_mla_flash_kernel, tq=tq),
        grid=(batch, MLA_HEADS // 2),
        in_specs=[pl.BlockSpec((None, seq, pair), lambda b, h: (b, 0, h))] * 3,
        out_specs=pl.BlockSpec((None, seq, 2 * V_DIM), lambda b, h: (b, 0, h)),
        out_shape=jax.ShapeDtypeStruct((batch, seq, MLA_HEADS * V_DIM), BF16),
        scratch_shapes=[pltpu.VMEM((2, tq, HEAD_BLOCK), F32), pltpu.VMEM((2, tq, HEAD_BLOCK), F32),
                        pltpu.VMEM((2, tq, tq), F32), pltpu.VMEM((2, tq, tq), F32)],
        compiler_params=_cparams("parallel", "parallel"),
        name="mla_flash",
    )(q, k, v)
    return out.reshape(batch * seq, MLA_HEADS * V_DIM)


def _mla_decode_kernel(pt_ref, ql_ref, qr_ref, newt_ref, *rest, chain_tiles, page, kv_lora, t_new):
    n_pages = sum(chain_tiles) - 1
    _mla_decode_compute(ql_ref, qr_ref, list(rest[:n_pages]) + [newt_ref], rest[n_pages], rest[n_pages + 1:],
                        chain_tiles, page, kv_lora, t_new)


def _mla_decode_compute(ql_ref, qr_ref, tiles, o_ref, scratch, chain_tiles, page, kv_lora, t_new):
    rows = ql_ref.shape[0]
    ql = ql_ref[...]
    qr = qr_ref[...]
    stats = []
    first = 0
    for g, n_tiles in enumerate(chain_tiles):
        lat_sc, kr_sc = scratch[2 * g], scratch[2 * g + 1]
        for i, ref in enumerate(tiles[first:first + n_tiles]):
            lat_sc[:, i * page:(i + 1) * page] = ref[:kv_lora, :].astype(BF16)
            kr_sc[:, i * page:(i + 1) * page] = ref[kv_lora:, :].astype(BF16)
        first += n_tiles
        s = (jnp.dot(ql, lat_sc[...], preferred_element_type=F32)
             + jnp.dot(qr, kr_sc[...], preferred_element_type=F32))
        if g == len(chain_tiles) - 1:
            n_keys = n_tiles * page
            col = lax.broadcasted_iota(jnp.int32, (rows, n_keys), 1)
            t_row = lax.shift_right_logical(lax.broadcasted_iota(jnp.int32, (rows, n_keys), 0),
                                            int(math.log2(MLA_HEADS)))
            j_new = col - (n_tiles - 1) * page
            s = jnp.where((j_new < 0) | ((j_new <= t_row) & (j_new < t_new)), s, NEG)
        m = jnp.max(s, axis=-1, keepdims=True)
        p = jnp.exp(s - m)
        l = jnp.sum(p, axis=-1, keepdims=True)
        pv = lax.dot_general(p.astype(BF16), lat_sc[...], NT_DIMS, preferred_element_type=F32)
        stats.append((m, l, pv))
    m_all = functools.reduce(jnp.maximum, [st[0] for st in stats])
    num = den = None
    for m, l, pv in stats:
        a = jnp.exp(m - m_all)
        num = a * pv if num is None else num + a * pv
        den = a * l if den is None else den + a * l
    o_ref[...] = (num / den).astype(o_ref.dtype)


def _mla_decode(q_lat, q_rope, new_t, cache_t, layer, page_table):
    bd, rows, kv_lora = q_lat.shape
    n_pages = page_table.shape[1]
    row_w, page = cache_t.shape[2], cache_t.shape[3]
    t_new = rows // MLA_HEADS
    chain_tiles = _decode_chains(n_pages)

    def page_spec(i):
        return pl.BlockSpec((None, None, row_w, page), lambda b, pt: (layer, pt[b, i], 0, 0))

    scratch = []
    for n_tiles in chain_tiles:
        scratch += [pltpu.VMEM((kv_lora, n_tiles * page), BF16), pltpu.VMEM((QK_ROPE, n_tiles * page), BF16)]
    grid_spec = pltpu.PrefetchScalarGridSpec(
        num_scalar_prefetch=1,
        grid=(bd,),
        in_specs=[
            pl.BlockSpec((None, rows, kv_lora), lambda b, pt: (b, 0, 0)),
            pl.BlockSpec((None, rows, QK_ROPE), lambda b, pt: (b, 0, 0)),
            pl.BlockSpec((None, row_w, page), lambda b, pt: (b, 0, 0)),
        ] + [page_spec(i) for i in range(n_pages)],
        out_specs=pl.BlockSpec((None, rows, kv_lora), lambda b, pt: (b, 0, 0)),
        scratch_shapes=scratch,
    )
    kern = functools.partial(_mla_decode_kernel, chain_tiles=chain_tiles, page=page, kv_lora=kv_lora, t_new=t_new)
    return pl.pallas_call(
        kern,
        grid_spec=grid_spec,
        out_shape=jax.ShapeDtypeStruct((bd, rows, kv_lora), BF16),
        compiler_params=_cparams("parallel"),
        name="mla_decode",
    )(page_table, q_lat, q_rope, new_t, *([cache_t] * n_pages))


def _decode_chains(n_pages):
    n_chains = 4 if n_pages >= 8 else 1
    base, extra = divmod(n_pages + 1, n_chains)
    return tuple(base + (1 if g >= n_chains - extra else 0) for g in range(n_chains))


def _ffn_decode_kernel(pt_ref, x_ref, g_ref, wg_ref, wu_ref, wo_ref, gf_ref, ql_ref, qr_ref, newt_ref, *rest,
                       chain_tiles, page, kv_lora, t_new, sub):
    n_pages = sum(chain_tiles) - 1
    y_ref, olat_ref = rest[n_pages], rest[n_pages + 1]
    _mla_decode_compute(ql_ref, qr_ref, list(rest[:n_pages]) + [newt_ref], olat_ref, rest[n_pages + 2:],
                        chain_tiles, page, kv_lora, t_new)
    _ffn_kernel(x_ref, g_ref, wg_ref, wu_ref, wo_ref, gf_ref, y_ref, final_norm=False, sub=sub)


def _ffn_decode(x, g, w_in, w_out, layer, half, g_final, q_lat, q_rope, new_t, cache_t, page_table, b0, tm):
    m, d = x.shape
    d_ff = w_out.shape[2]
    n_steps = m // tm
    bd, rows, kv_lora = q_lat.shape
    assert b0 + n_steps <= bd
    n_pages = page_table.shape[1]
    row_w, page = cache_t.shape[2], cache_t.shape[3]
    chain_tiles = _decode_chains(n_pages)
    resident = dict(pipeline_mode=pl.Buffered(1))

    def page_spec(j):
        return pl.BlockSpec((None, None, row_w, page), lambda i, pt: (layer, pt[b0 + i, j], 0, 0))

    scratch = []
    for n_tiles in chain_tiles:
        scratch += [pltpu.VMEM((kv_lora, n_tiles * page), BF16), pltpu.VMEM((QK_ROPE, n_tiles * page), BF16)]
    grid_spec = pltpu.PrefetchScalarGridSpec(
        num_scalar_prefetch=1,
        grid=(n_steps,),
        in_specs=[
            pl.BlockSpec((tm, d), lambda i, pt: (i, 0)),
            pl.BlockSpec((1, d), lambda i, pt: (0, 0)),
            pl.BlockSpec((None, None, d, d_ff), lambda i, pt: (layer, half, 0, 0), **resident),
            pl.BlockSpec((None, None, d, d_ff), lambda i, pt: (layer, half, 0, 1), **resident),
            pl.BlockSpec((None, None, d_ff, d), lambda i, pt: (layer, half, 0, 0), **resident),
            pl.BlockSpec((1, d), lambda i, pt: (0, 0)),
            pl.BlockSpec((None, rows, kv_lora), lambda i, pt: (b0 + i, 0, 0)),
            pl.BlockSpec((None, rows, QK_ROPE), lambda i, pt: (b0 + i, 0, 0)),
            pl.BlockSpec((None, row_w, page), lambda i, pt: (b0 + i, 0, 0)),
        ] + [page_spec(j) for j in range(n_pages)],
        out_specs=[pl.BlockSpec((tm, d), lambda i, pt: (i, 0)),
                   pl.BlockSpec((None, rows, kv_lora), lambda i, pt: (i, 0, 0))],
        scratch_shapes=scratch,
    )
    kern = functools.partial(_ffn_decode_kernel, chain_tiles=chain_tiles, page=page, kv_lora=kv_lora,
                             t_new=rows // MLA_HEADS, sub=min(d_ff, FFN_SUB_COLS))
    return pl.pallas_call(
        kern,
        grid_spec=grid_spec,
        out_shape=[jax.ShapeDtypeStruct((m, d), F32), jax.ShapeDtypeStruct((n_steps, rows, kv_lora), BF16)],
        compiler_params=_cparams("arbitrary"),
        name="ffn_decode",
    )(page_table, x, g.reshape(1, d), w_in, w_in, w_out, g_final.reshape(1, d), q_lat, q_rope, new_t,
      *([cache_t] * n_pages))


def _norm_rope_kernel(x_ref, g_ref, w_ref, c_ref, s1_ref, s2_ref, *refs, rope_blocks, group_dils, group_blocks,
                      n_f32_out):
    n_groups = 0 if group_dils is None else len(group_dils)
    n_out = (n_groups if group_dils is not None else 0) + n_f32_out
    out_refs = refs[:n_out]
    sc = refs[n_out] if group_dils is not None else None
    tm = x_ref.shape[0]
    h = _rms(x_ref[...], g_ref[...]).astype(BF16)
    y = jnp.dot(h, w_ref[...], preferred_element_type=F32)
    ct, s1, s2 = c_ref[...], s1_ref[...], s2_ref[...]
    half = ROT_DIM // 2
    for j, (roped, dest) in enumerate(rope_blocks):
        blk = y[:, j * LANES:(j + 1) * LANES]
        if roped:
            blk = blk * ct + pltpu.roll(blk, LANES - half, 1) * s1 + pltpu.roll(blk, half, 1) * s2
        if n_f32_out:
            out_refs[n_out - 1][:, dest * LANES:(dest + 1) * LANES] = blk
        if group_dils is not None:
            gi, jj = divmod(dest, group_blocks)
            dil = group_dils[gi]
            width = group_blocks * LANES
            if dil == 1:
                out_refs[gi][:, jj * LANES:(jj + 1) * LANES] = blk.astype(BF16)
            else:
                sc[...] = blk
                for r in range(dil):
                    lo = r * width + jj * LANES
                    out_refs[gi][:, lo:lo + LANES] = sc[pl.ds(r, tm // dil, stride=dil), :].astype(BF16)


def _norm_rope(x, g, w, tabs, rope_blocks, *, group_dils=None, group_blocks=None, emit_f32=False):
    m, d = x.shape
    n = w.shape[1]
    tm = _row_tile(m, 512)
    n_pos = tabs[0].shape[0] // tm
    tab_spec = pl.BlockSpec((tm, LANES), lambda i: (i % n_pos, 0))
    out_shape, out_specs, scratch = [], [], []
    if group_dils is not None:
        width = group_blocks * LANES
        for dil in group_dils:
            assert tm % (dil * 2 * SUBLANES) == 0
            out_shape.append(jax.ShapeDtypeStruct((m // dil, dil * width), BF16))
            out_specs.append(pl.BlockSpec((tm // dil, dil * width), lambda i: (i, 0)))
        scratch = [pltpu.VMEM((tm, LANES), F32)]
    else:
        emit_f32 = True
    if emit_f32:
        out_shape.append(jax.ShapeDtypeStruct((m, n), F32))
        out_specs.append(pl.BlockSpec((tm, n), lambda i: (i, 0)))
    return pl.pallas_call(
        functools.partial(_norm_rope_kernel, rope_blocks=tuple(rope_blocks), group_dils=group_dils,
                          group_blocks=group_blocks, n_f32_out=int(emit_f32)),
        grid=(m // tm,),
        in_specs=[pl.BlockSpec((tm, d), lambda i: (i, 0)), pl.BlockSpec((1, d), lambda i: (0, 0)),
                  pl.BlockSpec((d, n), lambda i: (0, 0)), tab_spec, tab_spec, tab_spec],
        out_specs=out_specs,
        out_shape=out_shape,
        scratch_shapes=scratch,
        compiler_params=_cparams("parallel"),
        name="norm_rope",
    )(x, g.reshape(1, d), w, *tabs)


def _dil_prompt_kernel(q_ref, kp_ref, kc_ref, vp_ref, vc_ref, o_ref, lse_ref, *, tq, n_sub):
    cblk = pl.program_id(2)
    lane = lax.broadcasted_iota(jnp.int32, (tq, LANES), 1)
    i_idx = lax.broadcasted_iota(jnp.int32, (2 * tq, 2 * tq), 0) & (tq - 1)
    j_idx = lax.broadcasted_iota(jnp.int32, (2 * tq, 2 * tq), 1)
    valid_cur = (j_idx >= tq) & (j_idx - tq <= i_idx)
    valid_mid = ((j_idx < tq) & (j_idx >= i_idx)) | valid_cur
    prev_off = jnp.where(cblk > 0, 0, tq)
    valid_first = ((j_idx < tq) & (j_idx >= i_idx + prev_off)) | valid_cur
    for sub in range(n_sub):
        rs = slice(sub * tq, (sub + 1) * tq)
        ps = slice((sub - 1) * tq, sub * tq)
        valid = valid_first if sub == 0 else valid_mid
        for hp in range(HEADS_PER_GROUP // 2):
            sl = slice(hp * LANES, (hp + 1) * LANES)
            q2 = q_ref[rs, sl]
            zero = jnp.zeros_like(q2)
            qs = jnp.concatenate([jnp.where(lane < HEAD_DIM, q2, zero), jnp.where(lane >= HEAD_DIM, q2, zero)],
                                 axis=0)
            k_prev = kp_ref[:, sl] if sub == 0 else kc_ref[ps, sl]
            v_prev = vp_ref[:, sl] if sub == 0 else vc_ref[ps, sl]
            kcat = jnp.concatenate([k_prev, kc_ref[rs, sl]], axis=0)
            vcat = jnp.concatenate([v_prev, vc_ref[rs, sl]], axis=0)
            s = jnp.where(valid, lax.dot_general(qs, kcat, NT_DIMS, preferred_element_type=F32), NEG)
            m = jnp.max(s, axis=-1, keepdims=True)
            p = jnp.exp(s - m)
            den = jnp.sum(p, axis=-1, keepdims=True)
            o2 = jnp.dot(p.astype(BF16), vcat, preferred_element_type=F32) / den
            lse2 = jnp.broadcast_to(m + jnp.log(den), (2 * tq, LANES))
            o_ref[rs, sl] = jnp.where(lane < HEAD_DIM, o2[:tq], o2[tq:])
            lse_ref[rs, sl] = jnp.where(lane < HEAD_DIM, lse2[:tq], lse2[tq:])


def _dil_prompt_group(q_g, kv_g, dil, batch, seq):
    assert seq % (dil * DIL_N) == 0
    rows = seq // dil
    tq = DIL_N
    n_sub = min(4, rows // tq)
    nblk = rows // (tq * n_sub)
    qv = q_g.reshape(batch, rows, dil * GROUP_WIDTH)
    kvv = kv_g.reshape(batch, rows, dil * 2 * GROUP_WIDTH)
    blk = (None, tq * n_sub, GROUP_WIDTH)
    pblk = (None, tq, GROUP_WIDTH)
    cur = lambda off: pl.BlockSpec(blk, lambda b, r, c: (b, c, 2 * r + off))
    prev = lambda off: pl.BlockSpec(pblk, lambda b, r, c: (b, jnp.maximum(c * n_sub - 1, 0), 2 * r + off))
    rspec = pl.BlockSpec(blk, lambda b, r, c: (b, c, r))
    o, lse = pl.pallas_call(
        functools.partial(_dil_prompt_kernel, tq=tq, n_sub=n_sub),
        grid=(batch, dil, nblk),
        in_specs=[rspec, prev(0), cur(0), prev(1), cur(1)],
        out_specs=[rspec, rspec],
        out_shape=[jax.ShapeDtypeStruct((batch, rows, dil * GROUP_WIDTH), F32)] * 2,
        compiler_params=_cparams("parallel", "parallel", "arbitrary"),
        name="dil_prompt",
    )(qv, kvv, kvv, kvv, kvv)
    return o.reshape(batch * rows, dil * GROUP_WIDTH), lse.reshape(batch * rows, dil * GROUP_WIDTH)


def _dil_sample_kernel(q_ref, buf_ref, new_ref, *rest, bb, dil, t_new, emit_state):
    if emit_state:
        o_ref, lse_ref, state_ref = rest
    else:
        o_ref, lse_ref = rest
    w = buf_ref.shape[2]
    rows = t_new * HEADS_PER_GROUP
    log_h = int(math.log2(HEADS_PER_GROUP))
    lane = lax.broadcasted_iota(jnp.int32, (rows, GROUP_WIDTH), 1)
    row = lax.broadcasted_iota(jnp.int32, (rows, GROUP_WIDTH), 0)
    diag = lax.shift_right_logical(lane, int(math.log2(HEAD_DIM))) == (row & (HEADS_PER_GROUP - 1))
    w_idx = lax.broadcasted_iota(jnp.int32, (rows, w), 1)
    t_of_row = lax.shift_right_logical(lax.broadcasted_iota(jnp.int32, (rows, w), 0), log_h)
    valid = (w_idx >= t_of_row) if dil == 1 else ((w_idx & (dil - 1)) == t_of_row)
    n_new = new_ref.shape[1]
    j_new = lax.broadcasted_iota(jnp.int32, (rows, n_new), 1)
    t_new_row = lax.shift_right_logical(lax.broadcasted_iota(jnp.int32, (rows, n_new), 0), log_h)
    valid_new = (j_new <= t_new_row) if dil == 1 else (j_new == t_new_row)
    shift_lane = lax.broadcasted_iota(jnp.int32, (2 * GROUP_WIDTH, LANES), 1)
    for b in range(bb):
        q = q_ref[b]
        qrep = jnp.concatenate(
            [jnp.broadcast_to(q[t:t + 1, :], (HEADS_PER_GROUP, GROUP_WIDTH)) for t in range(t_new)], axis=0)
        qbd = jnp.where(diag, qrep, 0.0).astype(BF16)
        kt = buf_ref[b, :GROUP_WIDTH, :].astype(BF16)
        vt = buf_ref[b, GROUP_WIDTH:, :].astype(BF16)
        k_new = new_ref[b, :, :GROUP_WIDTH].astype(BF16)
        v_new = new_ref[b, :, GROUP_WIDTH:].astype(BF16)
        s = jnp.where(valid, jnp.dot(qbd, kt, preferred_element_type=F32), NEG)
        s_new = jnp.where(valid_new, lax.dot_general(qbd, k_new, NT_DIMS, preferred_element_type=F32), NEG)
        m = jnp.maximum(jnp.max(s, axis=-1, keepdims=True), jnp.max(s_new, axis=-1, keepdims=True))
        p = jnp.exp(s - m)
        p_new = jnp.exp(s_new - m)
        den = jnp.sum(p, axis=-1, keepdims=True) + jnp.sum(p_new, axis=-1, keepdims=True)
        o = (lax.dot_general(p.astype(BF16), vt, NT_DIMS, preferred_element_type=F32)
             + jnp.dot(p_new.astype(BF16), v_new, preferred_element_type=F32)) / den
        lse = jnp.broadcast_to(m + jnp.log(den), (rows, GROUP_WIDTH))
        o_ref[b] = jnp.sum(jnp.where(diag, o, 0.0).reshape(t_new, HEADS_PER_GROUP, GROUP_WIDTH), axis=1)
        lse_ref[b] = jnp.sum(jnp.where(diag, lse, 0.0).reshape(t_new, HEADS_PER_GROUP, GROUP_WIDTH), axis=1)
        if emit_state:
            n_col = w // LANES
            nxt = pltpu.roll(buf_ref[b, :, 0:LANES], LANES - t_new, 1)
            for jc in range(n_col):
                cur = nxt
                if jc + 1 < n_col:
                    nxt = pltpu.roll(buf_ref[b, :, (jc + 1) * LANES:(jc + 2) * LANES], LANES - t_new, 1)
                    cur = jnp.where(shift_lane < LANES - t_new, cur, nxt)
                state_ref[b, :, jc * LANES:(jc + 1) * LANES] = cur
            new_rows = jnp.concatenate([new_ref[b], jnp.zeros((LANES - n_new, 2 * GROUP_WIDTH), F32)], axis=0)
            state_ref[b, :, w - t_new:w] = new_rows.T[:, :t_new]


def _dil_sample_group(q_g, buf_t, new_g, dil, emit_state):
    bd, t_new, _ = q_g.shape
    w = buf_t.shape[2]
    assert w == dil * DIL_N, "state buffer must hold exactly one window"
    assert dil == 1 or dil >= t_new
    kv_w = 2 * GROUP_WIDTH
    bb = _row_tile(bd, max(1, 1024 // w))
    spec3 = lambda a: pl.BlockSpec((bb,) + a.shape[1:], lambda i: (i, 0, 0))
    in_specs = [spec3(q_g), spec3(buf_t), spec3(new_g)]
    args = [q_g, buf_t, new_g]
    out_spec = pl.BlockSpec((bb, t_new, GROUP_WIDTH), lambda i: (i, 0, 0))
    out_specs = [out_spec, out_spec]
    out_shape = [jax.ShapeDtypeStruct((bd, t_new, GROUP_WIDTH), F32)] * 2
    if emit_state:
        out_specs.append(spec3(buf_t))
        out_shape.append(jax.ShapeDtypeStruct(buf_t.shape, F32))
    res = pl.pallas_call(
        functools.partial(_dil_sample_kernel, bb=bb, dil=dil, t_new=t_new, emit_state=emit_state),
        grid=(bd // bb,),
        in_specs=in_specs,
        out_specs=out_specs,
        out_shape=out_shape,
        compiler_params=_cparams("parallel"),
        name="dil_sample",
    )(*args)
    o, lse = res[0].reshape(bd * t_new, GROUP_WIDTH), res[1].reshape(bd * t_new, GROUP_WIDTH)
    return o, lse, (res[2] if emit_state else None)


def _dil_out_kernel(*refs, group_dils):
    n_g = N_GROUPS
    o_refs, l_refs = refs[:n_g], refs[n_g:2 * n_g]
    w_ref, x_ref, y_ref = refs[2 * n_g:2 * n_g + 3]
    scratch = refs[2 * n_g + 3:]
    tm = x_ref.shape[0]
    cols = []
    for jj in range(GROUP_LANE_BLOCKS):
        sl = slice(jj * LANES, (jj + 1) * LANES)
        os_, ls_ = [], []
        for gi in range(n_g):
            dil = 1 if group_dils is None else group_dils[gi]
            if dil == 1:
                os_.append(o_refs[gi][:, sl])
                ls_.append(l_refs[gi][:, sl])
            else:
                for src, dst in ((o_refs[gi], scratch[0]), (l_refs[gi], scratch[1])):
                    for r in range(dil):
                        lo = r * GROUP_WIDTH + jj * LANES
                        dst[pl.ds(r, tm // dil, stride=dil), :] = src[:, lo:lo + LANES]
                os_.append(scratch[0][...])
                ls_.append(scratch[1][...])
        m = jnp.maximum(jnp.maximum(ls_[0], ls_[1]), ls_[2])
        e = [jnp.exp(l - m) for l in ls_]
        den = e[0] + e[1] + e[2]
        cols.append(((e[0] / den) * os_[0] + (e[1] / den) * os_[1] + (e[2] / den) * os_[2]).astype(BF16))
    o = jnp.concatenate(cols, axis=1)
    y_ref[...] = x_ref[...] + jnp.dot(o, w_ref[...], preferred_element_type=F32)


def _dil_out(outs, lses, w_o, x, group_dils=None):
    m, d = x.shape
    tm = _row_tile(m, 512)
    specs = []
    for gi in range(N_GROUPS):
        dil = 1 if group_dils is None else group_dils[gi]
        specs.append(pl.BlockSpec((tm // dil, dil * GROUP_WIDTH), lambda i: (i, 0)))
    scratch = [] if group_dils is None else [pltpu.VMEM((tm, LANES), F32)] * 2
    return pl.pallas_call(
        functools.partial(_dil_out_kernel, group_dils=group_dils),
        grid=(m // tm,),
        in_specs=specs + specs + [pl.BlockSpec(w_o.shape, lambda i: (0, 0)), pl.BlockSpec((tm, d), lambda i: (i, 0))],
        out_specs=pl.BlockSpec((tm, d), lambda i: (i, 0)),
        out_shape=jax.ShapeDtypeStruct((m, d), F32),
        scratch_shapes=scratch,
        compiler_params=_cparams("parallel"),
        name="dil_out",
    )(*outs, *lses, w_o, x)


def _rope_cos_sin(pos, rot_dim, theta):
    half = rot_dim // 2
    inv = jnp.float32(theta) ** (-2.0 * jnp.arange(half, dtype=F32) / rot_dim)
    ang = pos[:, None] * inv[None, :]
    return jnp.cos(ang), jnp.sin(ang)


def _mla_tables(pos, q_scale):
    cos, sin = _rope_cos_sin(pos, QK_ROPE, MLA_THETA)
    n = pos.shape[0]
    z = lambda w: jnp.zeros((n, w), F32)
    cos2 = jnp.concatenate([cos, cos], axis=1)
    sin2 = jnp.concatenate([sin, sin], axis=1)
    rest = HEAD_BLOCK - QK_NOPE - QK_ROPE
    cq = jnp.concatenate([jnp.ones((n, QK_NOPE), F32), cos2, z(rest)], axis=1) * q_scale
    sq = jnp.concatenate([z(QK_NOPE), sin2, z(rest)], axis=1) * q_scale
    ck = jnp.concatenate([z(QK_NOPE), cos2, z(rest)], axis=1)
    sk = jnp.concatenate([z(QK_NOPE), sin2, z(rest)], axis=1)
    return cq, sq, ck, sk


def _dil_tables(pos, scale):
    cos, sin = _rope_cos_sin(pos, ROT_DIM, ROPE_THETA)
    n = pos.shape[0]
    half = ROT_DIM // 2
    z = lambda w: jnp.zeros((n, w), F32)
    keep = jnp.ones((n, HEAD_DIM - ROT_DIM), F32)
    c_head = jnp.concatenate([cos, cos, keep], axis=1)
    s1_head = jnp.concatenate([-sin, z(HEAD_DIM - half)], axis=1)
    s2_head = jnp.concatenate([z(half), sin, z(HEAD_DIM - ROT_DIM)], axis=1)
    two = lambda a: jnp.concatenate([a, a], axis=1) * scale
    return two(c_head), two(s1_head), two(s2_head)


def _rot_half_cols(w):
    half = w.shape[-1] // 2
    return jnp.concatenate([-w[..., half:], w[..., :half]], axis=-1)


def _prep_mla_weights(w_in, w_qb, w_ukv, q_lora, kv_lora):
    d = w_in.shape[0]
    w_kr = w_in[:, q_lora + kv_lora:]
    assert HEAD_BLOCK - QK_NOPE - QK_ROPE == QK_ROPE
    w_in_ext = jnp.concatenate(
        [w_in[:, :q_lora + kv_lora], jnp.zeros((d, QK_NOPE), F32), w_kr, _rot_half_cols(w_kr)], axis=1)
    wq = w_qb.reshape(q_lora, MLA_HEADS, QK_NOPE + QK_ROPE)
    wq_rope = wq[..., QK_NOPE:]
    w_qb_ext = jnp.concatenate([wq[..., :QK_NOPE], wq_rope, _rot_half_cols(wq_rope)], axis=-1)
    w_qb_ext = w_qb_ext.reshape(q_lora, MLA_HEADS * HEAD_BLOCK)
    wkv = w_ukv.reshape(kv_lora, MLA_HEADS, QK_NOPE + V_DIM)
    zpad = jnp.zeros((kv_lora, MLA_HEADS, HEAD_BLOCK - QK_NOPE), F32)
    wk = jnp.concatenate([wkv[..., :QK_NOPE], zpad], axis=-1).reshape(kv_lora, MLA_HEADS * HEAD_BLOCK)
    wv_even = jnp.concatenate([wkv[..., QK_NOPE:], zpad], axis=-1)
    wv_odd = jnp.concatenate([zpad, wkv[..., QK_NOPE:]], axis=-1)
    odd = (jnp.arange(MLA_HEADS) % 2 == 1)[None, :, None]
    wv = jnp.where(odd, wv_odd, wv_even).reshape(kv_lora, MLA_HEADS * HEAD_BLOCK)
    w_ukv_ext = jnp.concatenate([wk, wv], axis=1)
    one = np.zeros((1, MLA_HEADS, HEAD_BLOCK), np.float32)
    one[0, 0::2, V_DIM] = 1.0
    one[0, 1::2, 0] = 1.0
    v_one = jnp.asarray(one.reshape(1, MLA_HEADS * HEAD_BLOCK))
    eye = jnp.eye(MLA_HEADS, dtype=F32)
    w_uk = wkv[..., :QK_NOPE]
    w_uk_rows = jnp.concatenate([jnp.transpose(w_uk, (1, 2, 0)),
                                 jnp.zeros((MLA_HEADS, HEAD_BLOCK - QK_NOPE, kv_lora), F32)], axis=1)
    w_uk_bd = jnp.einsum('hnc,hg->hngc', w_uk_rows, eye).reshape(MLA_HEADS * HEAD_BLOCK, MLA_HEADS * kv_lora)
    w_uv = jnp.transpose(wkv[..., QK_NOPE:], (1, 0, 2))
    w_uv_bd = jnp.einsum('hcv,hg->hcgv', w_uv, eye).reshape(MLA_HEADS * kv_lora, MLA_HEADS * V_DIM)
    bf = lambda a: a.astype(BF16)
    return bf(w_in_ext), bf(w_qb_ext), bf(w_ukv_ext), v_one, bf(w_uk_bd), bf(w_uv_bd)


def kernel(x_prompt, x_sample, cache_mla, page_table, state_dil_w128, state_dil_w512, state_dil_w2048,
           g_layers, w_ffn_in, w_ffn_out, w_mla_in, g_mla_q, g_mla_kv, w_mla_qb, w_mla_ukv, w_mla_o,
           g_shared_kv, w_shared_kv, w_dil_q, w_dil_o, g_final):
    batch, seq, d_model = x_prompt.shape
    bd, t_new, _ = x_sample.shape
    depth = g_layers.shape[0]
    n_a = w_mla_in.shape[0]
    q_lora = g_mla_q.shape[1]
    kv_lora = g_mla_kv.shape[1]
    page = cache_mla.shape[2]
    past_len = page_table.shape[1] * page
    buffers = [state_dil_w128, state_dil_w512, state_dil_w2048]
    dils = tuple(dil for _, dil in DIL_GROUPS)
    kv_w = 2 * GROUP_WIDTH

    w_ffn_in_b = w_ffn_in.astype(BF16)
    w_ffn_out_b = w_ffn_out.astype(BF16)
    mla_w = [_prep_mla_weights(w_mla_in[a], w_mla_qb[a], w_mla_ukv[a], q_lora, kv_lora) for a in range(n_a)]
    w_mla_o_b = w_mla_o.astype(BF16)
    w_shared_b = w_shared_kv.astype(BF16)
    w_dil_q_b = w_dil_q.astype(BF16)
    w_dil_o_b = w_dil_o.astype(BF16)
    kv_rope_blocks = []
    for is_v in range(2):
        for gi in range(N_GROUPS):
            for jj in range(GROUP_LANE_BLOCKS):
                kv_rope_blocks.append((not is_v, (2 * gi + is_v) * GROUP_LANE_BLOCKS + jj))
    q_rope_blocks = [(True, j) for j in range(DIL_WIDTH // LANES)]

    pos_p = jnp.arange(seq, dtype=F32)
    pos_s = jnp.tile((past_len + jnp.arange(t_new)).astype(F32), bd)

    def ffn(x, l, half, final_norm=False):
        return _ffn(x, g_layers[l, 2 * half], w_ffn_in_b, w_ffn_out_b, l, half, g_final, final_norm=final_norm)

    mla_tabs_p = _mla_tables(pos_p, MLA_SCALE * LOG2E)
    dil_q_tabs_p = _dil_tables(pos_p, DIL_SCALE)
    dil_k_tabs_p = _dil_tables(pos_p, 1.0)
    mla_tabs_s = _mla_tables(pos_s, MLA_SCALE)
    cache_t = jnp.swapaxes(cache_mla, 2, 3)
    xs = x_sample.reshape(bd * t_new, d_model)
    rows_s = []
    host_tm = 256
    host_steps = (batch * seq) // host_tm if (batch * seq) % host_tm == 0 else 0
    hosted = bd == 2 * host_steps

    x = x_prompt.reshape(batch * seq, d_model)
    rows_p = []
    kv_groups = kv_f32_p = None
    for l in range(depth):
        if l < n_a:
            w_in_ext, w_qb_ext, w_ukv_ext, v_one, w_uk_bd, w_uv_bd = mla_w[l]
            xs = ffn(xs, l, 0)
            q_s, ckv_s, kr_s = _mla_proj(xs, g_layers[l, 1], w_in_ext, g_mla_q[l], g_mla_kv[l], w_qb_ext,
                                         w_ukv_ext, mla_tabs_s, v_one, with_kv=False)
            rows = jnp.concatenate([ckv_s, kr_s[:, QK_NOPE:QK_NOPE + QK_ROPE]], axis=1)
            rows_s.append(rows)
            q_lat = _linear(q_s, w_uk_bd, out_dtype=BF16).reshape(bd, t_new * MLA_HEADS, kv_lora)
            q_rope = q_s.reshape(bd, t_new * MLA_HEADS, HEAD_BLOCK)[:, :, QK_NOPE:QK_NOPE + QK_ROPE]
            new_pad = jnp.pad(rows.reshape(bd, t_new, kv_lora + QK_ROPE), ((0, 0), (0, page - t_new), (0, 0)))
            new_t = jnp.swapaxes(new_pad, 1, 2)
            decode_args = (q_lat, q_rope, new_t, cache_t, page_table)
            if hosted:
                x, o_lat_a = _ffn_decode(x, g_layers[l, 0], w_ffn_in_b, w_ffn_out_b, l, 0, g_final, *decode_args,
                                         b0=0, tm=host_tm)
            else:
                x = ffn(x, l, 0)
            q, ckv, kr, k, v = _mla_proj(x, g_layers[l, 1], w_in_ext, g_mla_q[l], g_mla_kv[l], w_qb_ext, w_ukv_ext,
                                         mla_tabs_p, v_one, with_kv=True)
            o = _mla_flash(q, k, v, batch, seq)
            x = _linear(o, w_mla_o_b[l], residual=x)
            rows_p.append(jnp.concatenate([ckv, kr[:, QK_NOPE:QK_NOPE + QK_ROPE]], axis=1))
            if hosted:
                x, o_lat_b = _ffn_decode(x, g_layers[l, 2], w_ffn_in_b, w_ffn_out_b, l, 1, g_final, *decode_args,
                                         b0=host_steps, tm=host_tm)
                o_lat = jnp.concatenate([o_lat_a, o_lat_b], axis=0)
            else:
                x = ffn(x, l, 1)
                o_lat = _mla_decode(q_lat, q_rope, new_t, cache_t, l, page_table)
            o_s = _linear(o_lat.reshape(bd * t_new, MLA_HEADS * kv_lora), w_uv_bd, out_dtype=BF16)
            xs = _linear(o_s, w_mla_o_b[l], residual=xs)
            xs = ffn(xs, l, 1)
            if l == n_a - 1:
                *kv_groups, kv_f32_p = _norm_rope(x, g_shared_kv, w_shared_b, dil_k_tabs_p, kv_rope_blocks,
                                                  group_dils=dils, group_blocks=2 * GROUP_LANE_BLOCKS, emit_f32=True)
            continue
        x = ffn(x, l, 0)
        q_groups = _norm_rope(x, g_layers[l, 1], w_dil_q_b[l - n_a], dil_q_tabs_p, q_rope_blocks,
                              group_dils=dils, group_blocks=GROUP_LANE_BLOCKS)
        res = [_dil_prompt_group(q_groups[gi], kv_groups[gi], dil, batch, seq) for gi, dil in enumerate(dils)]
        x = _dil_out([r[0] for r in res], [r[1] for r in res], w_dil_o_b[l - n_a], x, group_dils=dils)
        x = ffn(x, l, 1, final_norm=(l == depth - 1))
    y_prompt = x.reshape(batch, seq, d_model)
    mla_rows_prompt = jnp.stack(rows_p, axis=0).reshape(n_a, batch, seq, kv_lora + QK_ROPE)
    kv_p3 = kv_f32_p.reshape(batch, seq, N_GROUPS * kv_w)
    dil_p = []
    for gi, (win, _) in enumerate(DIL_GROUPS):
        keep = min(win, seq)
        tail = kv_p3[:, seq - keep:, gi * kv_w:(gi + 1) * kv_w]
        dil_p.append(tail.reshape(batch, keep, 2, HEADS_PER_GROUP, HEAD_DIM))

    dil_q_tabs_s = _dil_tables(pos_s, DIL_SCALE)
    dil_k_tabs_s = _dil_tables(pos_s, 1.0)
    bufs_t = []
    for buf in buffers:
        length = buf.shape[1]
        assert min(length, length + t_new) == length
        bufs_t.append(jnp.transpose(buf, (0, 2, 3, 4, 1)).reshape(bd, kv_w, length))
    new_rows_pad = SUBLANES
    x = xs
    kv_f32_s = _norm_rope(x, g_shared_kv, w_shared_b, dil_k_tabs_s, kv_rope_blocks)[0]
    kv_s = kv_f32_s.reshape(bd, t_new, N_GROUPS, kv_w)
    kv_new = jnp.pad(kv_s, ((0, 0), (0, new_rows_pad - t_new), (0, 0), (0, 0)))
    states_t = [None] * N_GROUPS
    for l in range(n_a, depth):
        x = ffn(x, l, 0)
        q = _norm_rope(x, g_layers[l, 1], w_dil_q_b[l - n_a], dil_q_tabs_s, q_rope_blocks)[0]
        qg = q.reshape(bd, t_new, N_GROUPS, GROUP_WIDTH)
        outs, lses = [], []
        for gi, dil in enumerate(dils):
            emit = l == n_a
            o, lse, st = _dil_sample_group(qg[:, :, gi], bufs_t[gi], kv_new[:, :, gi], dil, emit)
            if emit:
                states_t[gi] = st
            outs.append(o)
            lses.append(lse)
        x = _dil_out(outs, lses, w_dil_o_b[l - n_a], x)
        x = ffn(x, l, 1, final_norm=(l == depth - 1))
    y_sample = x.reshape(bd, t_new, d_model)
    mla_rows_sample = jnp.stack(rows_s, axis=0).reshape(n_a, bd, t_new, kv_lora + QK_ROPE)
    dil_s = []
    for gi, buf in enumerate(buffers):
        length = buf.shape[1]
        st = states_t[gi].reshape(bd, 2, HEADS_PER_GROUP, HEAD_DIM, length)
        dil_s.append(jnp.transpose(st, (0, 4, 1, 2, 3)))

    return (y_prompt, y_sample, mla_rows_prompt, mla_rows_sample,
            dil_p[0], dil_p[1], dil_p[2], dil_s[0], dil_s[1], dil_s[2])
```

```python
import functools
import math

import numpy as np
import jax
import jax.numpy as jnp
from jax import lax
from jax.experimental import pallas as pl
from jax.experimental.pallas import tpu as pltpu

F32 = jnp.float32
BF16 = jnp.bfloat16

MLA_HEADS = 16
QK_NOPE = 64
QK_ROPE = 32
V_DIM = 64
MLA_THETA = 10000.0
MLA_SCALE = (QK_NOPE + QK_ROPE) ** -0.5
DIL_GROUPS = ((128, 1), (512, 4), (2048, 16))
N_GROUPS = len(DIL_GROUPS)
HEADS_PER_GROUP = 8
HEAD_DIM = 64
GROUP_WIDTH = HEADS_PER_GROUP * HEAD_DIM
DIL_WIDTH = N_GROUPS * GROUP_WIDTH
ROT_DIM = HEAD_DIM // 4
ROPE_THETA = 500000.0
DIL_SCALE = HEAD_DIM ** -0.5
DIL_N = 128
EPS = 1e-6
NEG = -1e30
LOG2E = math.log2(math.e)

LANES = 128
SUBLANES = 8
HEAD_BLOCK = 128
GROUP_LANE_BLOCKS = GROUP_WIDTH // LANES
MXU_DIM = 256
FFN_SUB_COLS = 2 * MXU_DIM
VMEM_LIMIT_BYTES = 56 * 1024 * 1024
NT_DIMS = (((1,), (1,)), ((), ()))


def _cparams(*sem):
    return pltpu.CompilerParams(dimension_semantics=sem, vmem_limit_bytes=VMEM_LIMIT_BYTES)


def _row_tile(m, pref):
    t = min(m, pref)
    assert m % t == 0, (m, t)
    return t


def _rms(x, g):
    return x * lax.rsqrt(jnp.mean(x * x, axis=-1, keepdims=True) + EPS) * g


def _ffn_kernel(x_ref, g_ref, wg_ref, wu_ref, wo_ref, gf_ref, o_ref, *, final_norm, sub):
    x = x_ref[...]
    h = _rms(x, g_ref[...]).astype(BF16)
    d_ff = wg_ref.shape[1]
    part = None
    for lo in range(0, d_ff, sub):
        hi = min(lo + sub, d_ff)
        gate = jnp.dot(h, wg_ref[:, lo:hi], preferred_element_type=F32)
        up = jnp.dot(h, wu_ref[:, lo:hi], preferred_element_type=F32)
        act = (gate / (1.0 + jnp.exp(-gate)) * up).astype(BF16)
        contrib = jnp.dot(act, wo_ref[lo:hi, :], preferred_element_type=F32)
        part = contrib if part is None else part + contrib
    y = x + 0.5 * part
    if final_norm:
        y = _rms(y, gf_ref[...])
    o_ref[...] = y


def _ffn(x, g, w_in, w_out, layer, half, g_final, *, final_norm=False):
    m, d = x.shape
    d_ff = w_out.shape[2]
    tm = _row_tile(m, 512)
    kern = functools.partial(_ffn_kernel, final_norm=final_norm, sub=min(d_ff, FFN_SUB_COLS))
    resident = dict(pipeline_mode=pl.Buffered(1))
    return pl.pallas_call(
        kern,
        grid=(m // tm,),
        in_specs=[
            pl.BlockSpec((tm, d), lambda i: (i, 0)),
            pl.BlockSpec((1, d), lambda i: (0, 0)),
            pl.BlockSpec((None, None, d, d_ff), lambda i: (layer, half, 0, 0), **resident),
            pl.BlockSpec((None, None, d, d_ff), lambda i: (layer, half, 0, 1), **resident),
            pl.BlockSpec((None, None, d_ff, d), lambda i: (layer, half, 0, 0), **resident),
            pl.BlockSpec((1, d), lambda i: (0, 0)),
        ],
        out_specs=pl.BlockSpec((tm, d), lambda i: (i, 0)),
        out_shape=jax.ShapeDtypeStruct((m, d), F32),
        compiler_params=_cparams("parallel"),
        name="ffn",
    )(x, g.reshape(1, d), w_in, w_in, w_out, g_final.reshape(1, d))


def _linear_kernel(*refs, has_res):
    if has_res:
        a_ref, w_ref, r_ref, o_ref = refs
    else:
        a_ref, w_ref, o_ref = refs
    acc = jnp.dot(a_ref[...], w_ref[...], preferred_element_type=F32)
    if has_res:
        acc = acc + r_ref[...]
    o_ref[...] = acc.astype(o_ref.dtype)


def _linear(a, w, residual=None, out_dtype=F32):
    m, k = a.shape
    n = w.shape[1]
    tm = _row_tile(m, 512)
    tn = _row_tile(n, 1024)
    in_specs = [pl.BlockSpec((tm, k), lambda i, j: (i, 0)), pl.BlockSpec((k, tn), lambda i, j: (0, j))]
    args = [a, w]
    if residual is not None:
        in_specs.append(pl.BlockSpec((tm, tn), lambda i, j: (i, j)))
        args.append(residual)
    return pl.pallas_call(
        functools.partial(_linear_kernel, has_res=residual is not None),
        grid=(m // tm, n // tn),
        in_specs=in_specs,
        out_specs=pl.BlockSpec((tm, tn), lambda i, j: (i, j)),
        out_shape=jax.ShapeDtypeStruct((m, n), out_dtype),
        compiler_params=_cparams("parallel", "parallel"),
        name="linear",
    )(*args)


def _mla_proj_kernel(x_ref, g1_ref, win_ref, gq_ref, gkv_ref, wqb_ref, wukv_ref,
                     cq_ref, sq_ref, ck_ref, sk_ref, vone_ref,
                     q_ref, ckv_ref, kr_ref, *kv_refs, q_lora, kv_lora, with_kv):
    h = _rms(x_ref[...], g1_ref[...]).astype(BF16)
    comb = jnp.dot(h, win_ref[...], preferred_element_type=F32)
    cq = _rms(comb[:, :q_lora], gq_ref[...])
    ckv = _rms(comb[:, q_lora:q_lora + kv_lora], gkv_ref[...])
    blk = comb[:, q_lora + kv_lora:]
    kr = blk * ck_ref[...] + pltpu.roll(blk, HEAD_BLOCK - QK_ROPE, 1) * sk_ref[...]
    ckv_ref[...] = ckv
    kr_ref[...] = kr

    q = jnp.dot(cq.astype(BF16), wqb_ref[...], preferred_element_type=F32)
    cq_t = cq_ref[...]
    sq_t = sq_ref[...]
    for hd in range(MLA_HEADS):
        sl = slice(hd * HEAD_BLOCK, (hd + 1) * HEAD_BLOCK)
        qb = q[:, sl]
        q_ref[:, sl] = (qb * cq_t + pltpu.roll(qb, HEAD_BLOCK - QK_ROPE, 1) * sq_t).astype(BF16)

    if with_kv:
        k_ref, v_ref = kv_refs
        kv = jnp.dot(ckv.astype(BF16), wukv_ref[...], preferred_element_type=F32)
        width = MLA_HEADS * HEAD_BLOCK
        for hd in range(MLA_HEADS):
            sl = slice(hd * HEAD_BLOCK, (hd + 1) * HEAD_BLOCK)
            k_ref[:, sl] = (kv[:, sl] + kr).astype(BF16)
        v_ref[...] = (kv[:, width:] + vone_ref[...]).astype(BF16)


def _mla_proj(x, g1, w_in_ext, g_q, g_kv, w_qb_ext, w_ukv_ext, tabs, v_one, with_kv):
    m, d = x.shape
    q_lora = g_q.shape[0]
    kv_lora = g_kv.shape[0]
    tm = _row_tile(m, 512)
    width = MLA_HEADS * HEAD_BLOCK
    n_pos = tabs[0].shape[0] // tm
    full = lambda a: pl.BlockSpec(a.shape, lambda i: (0,) * a.ndim)
    tab_spec = pl.BlockSpec((tm, LANES), lambda i: (i % n_pos, 0))
    row = lambda n: pl.BlockSpec((tm, n), lambda i: (i, 0))
    out_shape = [jax.ShapeDtypeStruct((m, width), BF16), jax.ShapeDtypeStruct((m, kv_lora), F32),
                 jax.ShapeDtypeStruct((m, LANES), F32)]
    out_specs = [row(width), row(kv_lora), row(LANES)]
    if with_kv:
        out_shape += [jax.ShapeDtypeStruct((m, width), BF16), jax.ShapeDtypeStruct((m, width), BF16)]
        out_specs += [row(width), row(width)]
    g1 = g1.reshape(1, d)
    g_q = g_q.reshape(1, q_lora)
    g_kv = g_kv.reshape(1, kv_lora)
    return pl.pallas_call(
        functools.partial(_mla_proj_kernel, q_lora=q_lora, kv_lora=kv_lora, with_kv=with_kv),
        grid=(m // tm,),
        in_specs=[row(d), full(g1), full(w_in_ext), full(g_q), full(g_kv), full(w_qb_ext), full(w_ukv_ext),
                  tab_spec, tab_spec, tab_spec, tab_spec, full(v_one)],
        out_specs=out_specs,
        out_shape=out_shape,
        compiler_params=_cparams("parallel"),
        name="mla_proj",
    )(x, g1, w_in_ext, g_q, g_kv, w_qb_ext, w_ukv_ext, *tabs, v_one)


def _mla_flash_kernel(q_ref, k_ref, v_ref, o_ref, m_sc, acc_sc, sa_sc, sb_sc, *, tq):
    def q_block(qi, carry):
        _mla_flash_q_block(qi, q_ref, k_ref, v_ref, o_ref, m_sc, acc_sc, sa_sc, sb_sc, tq)
        return carry

    lax.fori_loop(0, q_ref.shape[0] // tq, q_block, 0)


def _mla_flash_q_block(qi, q_ref, k_ref, v_ref, o_ref, m_sc, acc_sc, sa_sc, sb_sc, tq):
    tk = tq
    q_start = pl.multiple_of(qi * tq, tq)
    heads = (slice(0, HEAD_BLOCK), slice(HEAD_BLOCK, 2 * HEAD_BLOCK))
    m_sc[...] = jnp.full(m_sc.shape, NEG, F32)
    acc_sc[...] = jnp.zeros(acc_sc.shape, F32)

    def scores(kc, hh):
        start = pl.multiple_of(kc * tk, tk)
        return lax.dot_general(q_ref[pl.ds(q_start, tq), heads[hh]], k_ref[pl.ds(start, tk), heads[hh]], NT_DIMS,
                               preferred_element_type=F32)

    def softmax_pv(kc, hh, s, masked):
        start = pl.multiple_of(kc * tk, tk)
        if masked:
            row = lax.broadcasted_iota(jnp.int32, (tq, tk), 0)
            col = lax.broadcasted_iota(jnp.int32, (tq, tk), 1)
            s = jnp.where(col <= row, s, NEG)
        m_prev = m_sc[hh]
        m_new = jnp.maximum(m_prev, jnp.max(s, axis=-1, keepdims=True))
        alpha = jnp.exp2(m_prev - m_new)
        p = jnp.exp2(s - jnp.concatenate([m_new] * (tk // LANES), axis=1))
        v = v_ref[pl.ds(start, tk), heads[hh]]
        acc_sc[hh] = acc_sc[hh] * alpha + jnp.dot(p.astype(BF16), v, preferred_element_type=F32)
        m_sc[hh] = m_new

    def chunk(kc, cur, nxt, masked):
        if nxt is not None:
            for hh in range(2):
                nxt[hh] = scores(kc + 1, hh)
        for hh in range(2):
            softmax_pv(kc, hh, cur[hh], masked)

    for hh in range(2):
        sa_sc[hh] = scores(0, hh)

    def body(quad, carry):
        for j in range(0, 4, 2):
            chunk(4 * quad + j, sa_sc, sb_sc, False)
            chunk(4 * quad + j + 1, sb_sc, sa_sc, False)
        return carry

    n_quads = lax.shift_right_logical(qi, 2)
    lax.fori_loop(0, n_quads, body, 0)

    @pl.when((qi & 2) != 0)
    def _():
        chunk(4 * n_quads, sa_sc, sb_sc, False)
        chunk(4 * n_quads + 1, sb_sc, sa_sc, False)

    odd = (qi & 1) == 1

    @pl.when(odd)
    def _():
        chunk(qi - 1, sa_sc, sb_sc, False)
        chunk(qi, sb_sc, None, True)

    @pl.when(jnp.logical_not(odd))
    def _():
        chunk(qi, sa_sc, None, True)

    lane = lax.broadcasted_iota(jnp.int32, (tq, HEAD_BLOCK), 1)
    acc0 = acc_sc[0]
    acc1 = acc_sc[1]
    out0 = acc0 / acc0[:, V_DIM:V_DIM + 1]
    out1 = acc1 / acc1[:, 0:1]
    o_ref[pl.ds(q_start, tq), :] = jnp.where(lane < V_DIM, out0, out1).astype(o_ref.dtype)


def _mla_flash(q, k, v, batch, seq):
    width = MLA_HEADS * HEAD_BLOCK
    q = q.reshape(batch, seq, width)
    k = k.reshape(batch, seq, width)
    v = v.reshape(batch, seq, width)
    tq = _row_tile(seq, 512)
    pair = 2 * HEAD_BLOCK
    out = pl.pallas_call(
        functools.partial(_mla_flash_kernel, tq=tq),
        grid=(batch, MLA_HEADS // 2),
        in_specs=[pl.BlockSpec((None, seq, pair), lambda b, h: (b, 0, h))] * 3,
        out_specs=pl.BlockSpec((None, seq, 2 * V_DIM), lambda b, h: (b, 0, h)),
        out_shape=jax.ShapeDtypeStruct((batch, seq, MLA_HEADS * V_DIM), BF16),
        scratch_shapes=[pltpu.VMEM((2, tq, HEAD_BLOCK), F32), pltpu.VMEM((2, tq, HEAD_BLOCK), F32),
                        pltpu.VMEM((2, tq, tq), F32), pltpu.VMEM((2, tq, tq), F32)],
        compiler_params=_cparams("parallel", "parallel"),
        name="mla_flash",
    )(q, k, v)
    return out.reshape(batch * seq, MLA_HEADS * V_DIM)


def _mla_decode_kernel(pt_ref, ql_ref, qr_ref, newt_ref, *rest, chain_tiles, page, kv_lora, t_new):
    n_pages = sum(chain_tiles) - 1
    _mla_decode_compute(ql_ref, qr_ref, list(rest[:n_pages]) + [newt_ref], rest[n_pages], rest[n_pages + 1:],
                        chain_tiles, page, kv_lora, t_new)


def _mla_decode_compute(ql_ref, qr_ref, tiles, o_ref, scratch, chain_tiles, page, kv_lora, t_new):
    rows = ql_ref.shape[0]
    ql = ql_ref[...]
    qr = qr_ref[...]
    stats = []
    first = 0
    for g, n_tiles in enumerate(chain_tiles):
        lat_sc, kr_sc = scratch[2 * g], scratch[2 * g + 1]
        for i, ref in enumerate(tiles[first:first + n_tiles]):
            lat_sc[:, i * page:(i + 1) * page] = ref[:kv_lora, :].astype(BF16)
            kr_sc[:, i * page:(i + 1) * page] = ref[kv_lora:, :].astype(BF16)
        first += n_tiles
        s = (jnp.dot(ql, lat_sc[...], preferred_element_type=F32)
             + jnp.dot(qr, kr_sc[...], preferred_element_type=F32))
        if g == len(chain_tiles) - 1:
            n_keys = n_tiles * page
            col = lax.broadcasted_iota(jnp.int32, (rows, n_keys), 1)
            t_row = lax.shift_right_logical(lax.broadcasted_iota(jnp.int32, (rows, n_keys), 0),
                                            int(math.log2(MLA_HEADS)))
            j_new = col - (n_tiles - 1) * page
            s = jnp.where((j_new < 0) | ((j_new <= t_row) & (j_new < t_new)), s, NEG)
        m = jnp.max(s, axis=-1, keepdims=True)
        p = jnp.exp(s - m)
        l = jnp.sum(p, axis=-1, keepdims=True)
        pv = lax.dot_general(p.astype(BF16), lat_sc[...], NT_DIMS, preferred_element_type=F32)
        stats.append((m, l, pv))
    m_all = functools.reduce(jnp.maximum, [st[0] for st in stats])
    num = den = None
    for m, l, pv in stats:
        a = jnp.exp(m - m_all)
        num = a * pv if num is None else num + a * pv
        den = a * l if den is None else den + a * l
    o_ref[...] = (num / den).astype(o_ref.dtype)


def _mla_decode(q_lat, q_rope, new_t, cache_t, layer, page_table):
    bd, rows, kv_lora = q_lat.shape
    n_pages = page_table.shape[1]
    row_w, page = cache_t.shape[2], cache_t.shape[3]
    t_new = rows // MLA_HEADS
    chain_tiles = _decode_chains(n_pages)

    def page_spec(i):
        return pl.BlockSpec((None, None, row_w, page), lambda b, pt: (layer, pt[b, i], 0, 0))

    scratch = []
    for n_tiles in chain_tiles:
        scratch += [pltpu.VMEM((kv_lora, n_tiles * page), BF16), pltpu.VMEM((QK_ROPE, n_tiles * page), BF16)]
    grid_spec = pltpu.PrefetchScalarGridSpec(
        num_scalar_prefetch=1,
        grid=(bd,),
        in_specs=[
            pl.BlockSpec((None, rows, kv_lora), lambda b, pt: (b, 0, 0)),
            pl.BlockSpec((None, rows, QK_ROPE), lambda b, pt: (b, 0, 0)),
            pl.BlockSpec((None, row_w, page), lambda b, pt: (b, 0, 0)),
        ] + [page_spec(i) for i in range(n_pages)],
        out_specs=pl.BlockSpec((None, rows, kv_lora), lambda b, pt: (b, 0, 0)),
        scratch_shapes=scratch,
    )
    kern = functools.partial(_mla_decode_kernel, chain_tiles=chain_tiles, page=page, kv_lora=kv_lora, t_new=t_new)
    return pl.pallas_call(
        kern,
        grid_spec=grid_spec,
        out_shape=jax.ShapeDtypeStruct((bd, rows, kv_lora), BF16),
        compiler_params=_cparams("parallel"),
        name="mla_decode",
    )(page_table, q_lat, q_rope, new_t, *([cache_t] * n_pages))


def _decode_chains(n_pages):
    n_chains = 4 if n_pages >= 8 else 1
    base, extra = divmod(n_pages + 1, n_chains)
    return tuple(base + (1 if g >= n_chains - extra else 0) for g in range(n_chains))


def _norm_rope_kernel(x_ref, g_ref, w_ref, c_ref, s1_ref, s2_ref, *refs, rope_blocks, group_dils, group_blocks,
                      n_f32_out):
    n_groups = 0 if group_dils is None else len(group_dils)
    n_out = (n_groups if group_dils is not None else 0) + n_f32_out
    out_refs = refs[:n_out]
    sc = refs[n_out] if group_dils is not None else None
    tm = x_ref.shape[0]
    h = _rms(x_ref[...], g_ref[...]).astype(BF16)
    y = jnp.dot(h, w_ref[...], preferred_element_type=F32)
    ct, s1, s2 = c_ref[...], s1_ref[...], s2_ref[...]
    half = ROT_DIM // 2
    for j, (roped, dest) in enumerate(rope_blocks):
        blk = y[:, j * LANES:(j + 1) * LANES]
        if roped:
            blk = blk * ct + pltpu.roll(blk, LANES - half, 1) * s1 + pltpu.roll(blk, half, 1) * s2
        if n_f32_out:
            out_refs[n_out - 1][:, dest * LANES:(dest + 1) * LANES] = blk
        if group_dils is not None:
            gi, jj = divmod(dest, group_blocks)
            dil = group_dils[gi]
            width = group_blocks * LANES
            if dil == 1:
                out_refs[gi][:, jj * LANES:(jj + 1) * LANES] = blk.astype(BF16)
            else:
                sc[...] = blk
                for r in range(dil):
                    lo = r * width + jj * LANES
                    out_refs[gi][:, lo:lo + LANES] = sc[pl.ds(r, tm // dil, stride=dil), :].astype(BF16)


def _norm_rope(x, g, w, tabs, rope_blocks, *, group_dils=None, group_blocks=None, emit_f32=False):
    m, d = x.shape
    n = w.shape[1]
    tm = _row_tile(m, 512)
    n_pos = tabs[0].shape[0] // tm
    tab_spec = pl.BlockSpec((tm, LANES), lambda i: (i % n_pos, 0))
    out_shape, out_specs, scratch = [], [], []
    if group_dils is not None:
        width = group_blocks * LANES
        for dil in group_dils:
            assert tm % (dil * 2 * SUBLANES) == 0
            out_shape.append(jax.ShapeDtypeStruct((m // dil, dil * width), BF16))
            out_specs.append(pl.BlockSpec((tm // dil, dil * width), lambda i: (i, 0)))
        scratch = [pltpu.VMEM((tm, LANES), F32)]
    else:
        emit_f32 = True
    if emit_f32:
        out_shape.append(jax.ShapeDtypeStruct((m, n), F32))
        out_specs.append(pl.BlockSpec((tm, n), lambda i: (i, 0)))
    return pl.pallas_call(
        functools.partial(_norm_rope_kernel, rope_blocks=tuple(rope_blocks), group_dils=group_dils,
                          group_blocks=group_blocks, n_f32_out=int(emit_f32)),
        grid=(m // tm,),
        in_specs=[pl.BlockSpec((tm, d), lambda i: (i, 0)), pl.BlockSpec((1, d), lambda i: (0, 0)),
                  pl.BlockSpec((d, n), lambda i: (0, 0)), tab_spec, tab_spec, tab_spec],
        out_specs=out_specs,
        out_shape=out_shape,
        scratch_shapes=scratch,
        compiler_params=_cparams("parallel"),
        name="norm_rope",
    )(x, g.reshape(1, d), w, *tabs)


def _dil_prompt_kernel(q_ref, kp_ref, kc_ref, vp_ref, vc_ref, o_ref, lse_ref, *, tq, n_sub):
    cblk = pl.program_id(2)
    lane = lax.broadcasted_iota(jnp.int32, (tq, LANES), 1)
    i_idx = lax.broadcasted_iota(jnp.int32, (2 * tq, 2 * tq), 0) & (tq - 1)
    j_idx = lax.broadcasted_iota(jnp.int32, (2 * tq, 2 * tq), 1)
    valid_cur = (j_idx >= tq) & (j_idx - tq <= i_idx)
    valid_mid = ((j_idx < tq) & (j_idx >= i_idx)) | valid_cur
    prev_off = jnp.where(cblk > 0, 0, tq)
    valid_first = ((j_idx < tq) & (j_idx >= i_idx + prev_off)) | valid_cur
    for sub in range(n_sub):
        rs = slice(sub * tq, (sub + 1) * tq)
        ps = slice((sub - 1) * tq, sub * tq)
        valid = valid_first if sub == 0 else valid_mid
        for hp in range(HEADS_PER_GROUP // 2):
            sl = slice(hp * LANES, (hp + 1) * LANES)
            q2 = q_ref[rs, sl]
            zero = jnp.zeros_like(q2)
            qs = jnp.concatenate([jnp.where(lane < HEAD_DIM, q2, zero), jnp.where(lane >= HEAD_DIM, q2, zero)],
                                 axis=0)
            k_prev = kp_ref[:, sl] if sub == 0 else kc_ref[ps, sl]
            v_prev = vp_ref[:, sl] if sub == 0 else vc_ref[ps, sl]
            kcat = jnp.concatenate([k_prev, kc_ref[rs, sl]], axis=0)
            vcat = jnp.concatenate([v_prev, vc_ref[rs, sl]], axis=0)
            s = jnp.where(valid, lax.dot_general(qs, kcat, NT_DIMS, preferred_element_type=F32), NEG)
            m = jnp.max(s, axis=-1, keepdims=True)
            p = jnp.exp(s - m)
            den = jnp.sum(p, axis=-1, keepdims=True)
            o2 = jnp.dot(p.astype(BF16), vcat, preferred_element_type=F32) / den
            lse2 = jnp.broadcast_to(m + jnp.log(den), (2 * tq, LANES))
            o_ref[rs, sl] = jnp.where(lane < HEAD_DIM, o2[:tq], o2[tq:]).astype(o_ref.dtype)
            lse_ref[rs, sl] = jnp.where(lane < HEAD_DIM, lse2[:tq], lse2[tq:])


def _dil_prompt_group(q_g, kv_g, dil, batch, seq):
    assert seq % (dil * DIL_N) == 0
    rows = seq // dil
    tq = DIL_N
    n_sub = min(4, rows // tq)
    nblk = rows // (tq * n_sub)
    qv = q_g.reshape(batch, rows, dil * GROUP_WIDTH)
    kvv = kv_g.reshape(batch, rows, dil * 2 * GROUP_WIDTH)
    blk = (None, tq * n_sub, GROUP_WIDTH)
    pblk = (None, tq, GROUP_WIDTH)
    cur = lambda off: pl.BlockSpec(blk, lambda b, r, c: (b, c, 2 * r + off))
    prev = lambda off: pl.BlockSpec(pblk, lambda b, r, c: (b, jnp.maximum(c * n_sub - 1, 0), 2 * r + off))
    rspec = pl.BlockSpec(blk, lambda b, r, c: (b, c, r))
    o, lse = pl.pallas_call(
        functools.partial(_dil_prompt_kernel, tq=tq, n_sub=n_sub),
        grid=(batch, dil, nblk),
        in_specs=[rspec, prev(0), cur(0), prev(1), cur(1)],
        out_specs=[rspec, rspec],
        out_shape=[jax.ShapeDtypeStruct((batch, rows, dil * GROUP_WIDTH), BF16),
                   jax.ShapeDtypeStruct((batch, rows, dil * GROUP_WIDTH), F32)],
        compiler_params=_cparams("parallel", "parallel", "arbitrary"),
        name="dil_prompt",
    )(qv, kvv, kvv, kvv, kvv)
    return o.reshape(batch * rows, dil * GROUP_WIDTH), lse.reshape(batch * rows, dil * GROUP_WIDTH)


def _dil_sample_kernel(q_ref, buf_ref, new_ref, *rest, bb, dil, t_new, emit_state):
    if emit_state:
        o_ref, lse_ref, state_ref = rest
    else:
        o_ref, lse_ref = rest
    w = buf_ref.shape[2]
    rows = t_new * HEADS_PER_GROUP
    log_h = int(math.log2(HEADS_PER_GROUP))
    lane = lax.broadcasted_iota(jnp.int32, (rows, GROUP_WIDTH), 1)
    row = lax.broadcasted_iota(jnp.int32, (rows, GROUP_WIDTH), 0)
    diag = lax.shift_right_logical(lane, int(math.log2(HEAD_DIM))) == (row & (HEADS_PER_GROUP - 1))
    w_idx = lax.broadcasted_iota(jnp.int32, (rows, w), 1)
    t_of_row = lax.shift_right_logical(lax.broadcasted_iota(jnp.int32, (rows, w), 0), log_h)
    valid = (w_idx >= t_of_row) if dil == 1 else ((w_idx & (dil - 1)) == t_of_row)
    n_new = new_ref.shape[1]
    j_new = lax.broadcasted_iota(jnp.int32, (rows, n_new), 1)
    t_new_row = lax.shift_right_logical(lax.broadcasted_iota(jnp.int32, (rows, n_new), 0), log_h)
    valid_new = (j_new <= t_new_row) if dil == 1 else (j_new == t_new_row)
    shift_lane = lax.broadcasted_iota(jnp.int32, (2 * GROUP_WIDTH, LANES), 1)
    for b in range(bb):
        q = q_ref[b]
        qrep = jnp.concatenate(
            [jnp.broadcast_to(q[t:t + 1, :], (HEADS_PER_GROUP, GROUP_WIDTH)) for t in range(t_new)], axis=0)
        qbd = jnp.where(diag, qrep, 0.0).astype(BF16)
        kt = buf_ref[b, :GROUP_WIDTH, :].astype(BF16)
        vt = buf_ref[b, GROUP_WIDTH:, :].astype(BF16)
        k_new = new_ref[b, :, :GROUP_WIDTH].astype(BF16)
        v_new = new_ref[b, :, GROUP_WIDTH:].astype(BF16)
        s = jnp.where(valid, jnp.dot(qbd, kt, preferred_element_type=F32), NEG)
        s_new = jnp.where(valid_new, lax.dot_general(qbd, k_new, NT_DIMS, preferred_element_type=F32), NEG)
        m = jnp.maximum(jnp.max(s, axis=-1, keepdims=True), jnp.max(s_new, axis=-1, keepdims=True))
        p = jnp.exp(s - m)
        p_new = jnp.exp(s_new - m)
        den = jnp.sum(p, axis=-1, keepdims=True) + jnp.sum(p_new, axis=-1, keepdims=True)
        o = (lax.dot_general(p.astype(BF16), vt, NT_DIMS, preferred_element_type=F32)
             + jnp.dot(p_new.astype(BF16), v_new, preferred_element_type=F32)) / den
        lse = jnp.broadcast_to(m + jnp.log(den), (rows, GROUP_WIDTH))
        o_ref[b] = jnp.sum(jnp.where(diag, o, 0.0).reshape(t_new, HEADS_PER_GROUP, GROUP_WIDTH), axis=1)
        lse_ref[b] = jnp.sum(jnp.where(diag, lse, 0.0).reshape(t_new, HEADS_PER_GROUP, GROUP_WIDTH), axis=1)
        if emit_state:
            n_col = w // LANES
            nxt = pltpu.roll(buf_ref[b, :, 0:LANES], LANES - t_new, 1)
            for jc in range(n_col):
                cur = nxt
                if jc + 1 < n_col:
                    nxt = pltpu.roll(buf_ref[b, :, (jc + 1) * LANES:(jc + 2) * LANES], LANES - t_new, 1)
                    cur = jnp.where(shift_lane < LANES - t_new, cur, nxt)
                state_ref[b, :, jc * LANES:(jc + 1) * LANES] = cur
            new_rows = jnp.concatenate([new_ref[b], jnp.zeros((LANES - n_new, 2 * GROUP_WIDTH), F32)], axis=0)
            state_ref[b, :, w - t_new:w] = new_rows.T[:, :t_new]


def _dil_sample_group(q_g, buf_t, new_g, dil, emit_state):
    bd, t_new, _ = q_g.shape
    w = buf_t.shape[2]
    assert w == dil * DIL_N, "state buffer must hold exactly one window"
    assert dil == 1 or dil >= t_new
    kv_w = 2 * GROUP_WIDTH
    bb = _row_tile(bd, max(1, 1024 // w))
    spec3 = lambda a: pl.BlockSpec((bb,) + a.shape[1:], lambda i: (i, 0, 0))
    in_specs = [spec3(q_g), spec3(buf_t), spec3(new_g)]
    args = [q_g, buf_t, new_g]
    out_spec = pl.BlockSpec((bb, t_new, GROUP_WIDTH), lambda i: (i, 0, 0))
    out_specs = [out_spec, out_spec]
    out_shape = [jax.ShapeDtypeStruct((bd, t_new, GROUP_WIDTH), F32)] * 2
    if emit_state:
        out_specs.append(spec3(buf_t))
        out_shape.append(jax.ShapeDtypeStruct(buf_t.shape, F32))
    res = pl.pallas_call(
        functools.partial(_dil_sample_kernel, bb=bb, dil=dil, t_new=t_new, emit_state=emit_state),
        grid=(bd // bb,),
        in_specs=in_specs,
        out_specs=out_specs,
        out_shape=out_shape,
        compiler_params=_cparams("parallel"),
        name="dil_sample",
    )(*args)
    o, lse = res[0].reshape(bd * t_new, GROUP_WIDTH), res[1].reshape(bd * t_new, GROUP_WIDTH)
    return o, lse, (res[2] if emit_state else None)


def _dil_out_kernel(*refs, group_dils):
    n_g = N_GROUPS
    o_refs, l_refs = refs[:n_g], refs[n_g:2 * n_g]
    w_ref, x_ref, y_ref = refs[2 * n_g:2 * n_g + 3]
    scratch = refs[2 * n_g + 3:]
    tm = x_ref.shape[0]
    cols = []
    for jj in range(GROUP_LANE_BLOCKS):
        sl = slice(jj * LANES, (jj + 1) * LANES)
        os_, ls_ = [], []
        for gi in range(n_g):
            dil = 1 if group_dils is None else group_dils[gi]
            if dil == 1:
                os_.append(o_refs[gi][:, sl].astype(F32))
                ls_.append(l_refs[gi][:, sl])
            else:
                for src, dst in ((o_refs[gi], scratch[0]), (l_refs[gi], scratch[1])):
                    for r in range(dil):
                        lo = r * GROUP_WIDTH + jj * LANES
                        dst[pl.ds(r, tm // dil, stride=dil), :] = src[:, lo:lo + LANES].astype(F32)
                os_.append(scratch[0][...])
                ls_.append(scratch[1][...])
        m = jnp.maximum(jnp.maximum(ls_[0], ls_[1]), ls_[2])
        e = [jnp.exp(l - m) for l in ls_]
        den = e[0] + e[1] + e[2]
        cols.append(((e[0] / den) * os_[0] + (e[1] / den) * os_[1] + (e[2] / den) * os_[2]).astype(BF16))
    o = jnp.concatenate(cols, axis=1)
    y_ref[...] = x_ref[...] + jnp.dot(o, w_ref[...], preferred_element_type=F32)


def _dil_out(outs, lses, w_o, x, group_dils=None):
    m, d = x.shape
    tm = _row_tile(m, 512)
    specs = []
    for gi in range(N_GROUPS):
        dil = 1 if group_dils is None else group_dils[gi]
        specs.append(pl.BlockSpec((tm // dil, dil * GROUP_WIDTH), lambda i: (i, 0)))
    scratch = [] if group_dils is None else [pltpu.VMEM((tm, LANES), F32)] * 2
    return pl.pallas_call(
        functools.partial(_dil_out_kernel, group_dils=group_dils),
        grid=(m // tm,),
        in_specs=specs + specs + [pl.BlockSpec(w_o.shape, lambda i: (0, 0)), pl.BlockSpec((tm, d), lambda i: (i, 0))],
        out_specs=pl.BlockSpec((tm, d), lambda i: (i, 0)),
        out_shape=jax.ShapeDtypeStruct((m, d), F32),
        scratch_shapes=scratch,
        compiler_params=_cparams("parallel"),
        name="dil_out",
    )(*outs, *lses, w_o, x)


def _rope_cos_sin(pos, rot_dim, theta):
    half = rot_dim // 2
    inv = jnp.float32(theta) ** (-2.0 * jnp.arange(half, dtype=F32) / rot_dim)
    ang = pos[:, None] * inv[None, :]
    return jnp.cos(ang), jnp.sin(ang)


def _mla_tables(pos, q_scale):
    cos, sin = _rope_cos_sin(pos, QK_ROPE, MLA_THETA)
    n = pos.shape[0]
    z = lambda w: jnp.zeros((n, w), F32)
    cos2 = jnp.concatenate([cos, cos], axis=1)
    sin2 = jnp.concatenate([sin, sin], axis=1)
    rest = HEAD_BLOCK - QK_NOPE - QK_ROPE
    cq = jnp.concatenate([jnp.ones((n, QK_NOPE), F32), cos2, z(rest)], axis=1) * q_scale
    sq = jnp.concatenate([z(QK_NOPE), sin2, z(rest)], axis=1) * q_scale
    ck = jnp.concatenate([z(QK_NOPE), cos2, z(rest)], axis=1)
    sk = jnp.concatenate([z(QK_NOPE), sin2, z(rest)], axis=1)
    return cq, sq, ck, sk


def _dil_tables(pos, scale):
    cos, sin = _rope_cos_sin(pos, ROT_DIM, ROPE_THETA)
    n = pos.shape[0]
    half = ROT_DIM // 2
    z = lambda w: jnp.zeros((n, w), F32)
    keep = jnp.ones((n, HEAD_DIM - ROT_DIM), F32)
    c_head = jnp.concatenate([cos, cos, keep], axis=1)
    s1_head = jnp.concatenate([-sin, z(HEAD_DIM - half)], axis=1)
    s2_head = jnp.concatenate([z(half), sin, z(HEAD_DIM - ROT_DIM)], axis=1)
    two = lambda a: jnp.concatenate([a, a], axis=1) * scale
    return two(c_head), two(s1_head), two(s2_head)


def _rot_half_cols(w):
    half = w.shape[-1] // 2
    return jnp.concatenate([-w[..., half:], w[..., :half]], axis=-1)


def _prep_mla_weights(w_in, w_qb, w_ukv, q_lora, kv_lora):
    d = w_in.shape[0]
    w_kr = w_in[:, q_lora + kv_lora:]
    assert HEAD_BLOCK - QK_NOPE - QK_ROPE == QK_ROPE
    w_in_ext = jnp.concatenate(
        [w_in[:, :q_lora + kv_lora], jnp.zeros((d, QK_NOPE), F32), w_kr, _rot_half_cols(w_kr)], axis=1)
    wq = w_qb.reshape(q_lora, MLA_HEADS, QK_NOPE + QK_ROPE)
    wq_rope = wq[..., QK_NOPE:]
    w_qb_ext = jnp.concatenate([wq[..., :QK_NOPE], wq_rope, _rot_half_cols(wq_rope)], axis=-1)
    w_qb_ext = w_qb_ext.reshape(q_lora, MLA_HEADS * HEAD_BLOCK)
    wkv = w_ukv.reshape(kv_lora, MLA_HEADS, QK_NOPE + V_DIM)
    zpad = jnp.zeros((kv_lora, MLA_HEADS, HEAD_BLOCK - QK_NOPE), F32)
    wk = jnp.concatenate([wkv[..., :QK_NOPE], zpad], axis=-1).reshape(kv_lora, MLA_HEADS * HEAD_BLOCK)
    wv_even = jnp.concatenate([wkv[..., QK_NOPE:], zpad], axis=-1)
    wv_odd = jnp.concatenate([zpad, wkv[..., QK_NOPE:]], axis=-1)
    odd = (jnp.arange(MLA_HEADS) % 2 == 1)[None, :, None]
    wv = jnp.where(odd, wv_odd, wv_even).reshape(kv_lora, MLA_HEADS * HEAD_BLOCK)
    w_ukv_ext = jnp.concatenate([wk, wv], axis=1)
    one = np.zeros((1, MLA_HEADS, HEAD_BLOCK), np.float32)
    one[0, 0::2, V_DIM] = 1.0
    one[0, 1::2, 0] = 1.0
    v_one = jnp.asarray(one.reshape(1, MLA_HEADS * HEAD_BLOCK))
    eye = jnp.eye(MLA_HEADS, dtype=F32)
    w_uk = wkv[..., :QK_NOPE]
    w_uk_rows = jnp.concatenate([jnp.transpose(w_uk, (1, 2, 0)),
                                 jnp.zeros((MLA_HEADS, HEAD_BLOCK - QK_NOPE, kv_lora), F32)], axis=1)
    w_uk_bd = jnp.einsum('hnc,hg->hngc', w_uk_rows, eye).reshape(MLA_HEADS * HEAD_BLOCK, MLA_HEADS * kv_lora)
    w_uv = jnp.transpose(wkv[..., QK_NOPE:], (1, 0, 2))
    w_uv_bd = jnp.einsum('hcv,hg->hcgv', w_uv, eye).reshape(MLA_HEADS * kv_lora, MLA_HEADS * V_DIM)
    bf = lambda a: a.astype(BF16)
    return bf(w_in_ext), bf(w_qb_ext), bf(w_ukv_ext), v_one, bf(w_uk_bd), bf(w_uv_bd)


def kernel(x_prompt, x_sample, cache_mla, page_table, state_dil_w128, state_dil_w512, state_dil_w2048,
           g_layers, w_ffn_in, w_ffn_out, w_mla_in, g_mla_q, g_mla_kv, w_mla_qb, w_mla_ukv, w_mla_o,
           g_shared_kv, w_shared_kv, w_dil_q, w_dil_o, g_final):
    batch, seq, d_model = x_prompt.shape
    bd, t_new, _ = x_sample.shape
    depth = g_layers.shape[0]
    n_a = w_mla_in.shape[0]
    q_lora = g_mla_q.shape[1]
    kv_lora = g_mla_kv.shape[1]
    page = cache_mla.shape[2]
    past_len = page_table.shape[1] * page
    buffers = [state_dil_w128, state_dil_w512, state_dil_w2048]
    dils = tuple(dil for _, dil in DIL_GROUPS)
    kv_w = 2 * GROUP_WIDTH

    w_ffn_in_b = w_ffn_in.astype(BF16)
    w_ffn_out_b = w_ffn_out.astype(BF16)
    mla_w = [_prep_mla_weights(w_mla_in[a], w_mla_qb[a], w_mla_ukv[a], q_lora, kv_lora) for a in range(n_a)]
    w_mla_o_b = w_mla_o.astype(BF16)
    w_shared_b = w_shared_kv.astype(BF16)
    w_dil_q_b = w_dil_q.astype(BF16)
    w_dil_o_b = w_dil_o.astype(BF16)
    kv_rope_blocks = []
    for is_v in range(2):
        for gi in range(N_GROUPS):
            for jj in range(GROUP_LANE_BLOCKS):
                kv_rope_blocks.append((not is_v, (2 * gi + is_v) * GROUP_LANE_BLOCKS + jj))
    q_rope_blocks = [(True, j) for j in range(DIL_WIDTH // LANES)]

    pos_p = jnp.arange(seq, dtype=F32)
    pos_s = jnp.tile((past_len + jnp.arange(t_new)).astype(F32), bd)

    def ffn(x, l, half, final_norm=False):
        return _ffn(x, g_layers[l, 2 * half], w_ffn_in_b, w_ffn_out_b, l, half, g_final, final_norm=final_norm)

    mla_tabs_p = _mla_tables(pos_p, MLA_SCALE * LOG2E)
    dil_q_tabs_p = _dil_tables(pos_p, DIL_SCALE)
    dil_k_tabs_p = _dil_tables(pos_p, 1.0)
    mla_tabs_s = _mla_tables(pos_s, MLA_SCALE)
    cache_t = jnp.swapaxes(cache_mla, 2, 3)
    xs = x_sample.reshape(bd * t_new, d_model)
    rows_s = []

    x = x_prompt.reshape(batch * seq, d_model)
    rows_p = []
    kv_groups = kv_f32_p = None
    for l in range(depth):
        if l < n_a:
            w_in_ext, w_qb_ext, w_ukv_ext, v_one, w_uk_bd, w_uv_bd = mla_w[l]
            xs = ffn(xs, l, 0)
            q_s, ckv_s, kr_s = _mla_proj(xs, g_layers[l, 1], w_in_ext, g_mla_q[l], g_mla_kv[l], w_qb_ext,
                                         w_ukv_ext, mla_tabs_s, v_one, with_kv=False)
            rows = jnp.concatenate([ckv_s, kr_s[:, QK_NOPE:QK_NOPE + QK_ROPE]], axis=1)
            rows_s.append(rows)
            q_lat = _linear(q_s, w_uk_bd, out_dtype=BF16).reshape(bd, t_new * MLA_HEADS, kv_lora)
            q_rope = q_s.reshape(bd, t_new * MLA_HEADS, HEAD_BLOCK)[:, :, QK_NOPE:QK_NOPE + QK_ROPE]
            new_pad = jnp.pad(rows.reshape(bd, t_new, kv_lora + QK_ROPE), ((0, 0), (0, page - t_new), (0, 0)))
            new_t = jnp.swapaxes(new_pad, 1, 2)
            o_lat = _mla_decode(q_lat, q_rope, new_t, cache_t, l, page_table)
            x = ffn(x, l, 0)
            q, ckv, kr, k, v = _mla_proj(x, g_layers[l, 1], w_in_ext, g_mla_q[l], g_mla_kv[l], w_qb_ext, w_ukv_ext,
                                         mla_tabs_p, v_one, with_kv=True)
            o = _mla_flash(q, k, v, batch, seq)
            x = _linear(o, w_mla_o_b[l], residual=x)
            rows_p.append(jnp.concatenate([ckv, kr[:, QK_NOPE:QK_NOPE + QK_ROPE]], axis=1))
            x = ffn(x, l, 1)
            o_s = _linear(o_lat.reshape(bd * t_new, MLA_HEADS * kv_lora), w_uv_bd, out_dtype=BF16)
            xs = _linear(o_s, w_mla_o_b[l], residual=xs)
            xs = ffn(xs, l, 1)
            if l == n_a - 1:
                *kv_groups, kv_f32_p = _norm_rope(x, g_shared_kv, w_shared_b, dil_k_tabs_p, kv_rope_blocks,
                                                  group_dils=dils, group_blocks=2 * GROUP_LANE_BLOCKS, emit_f32=True)
            continue
        x = ffn(x, l, 0)
        q_groups = _norm_rope(x, g_layers[l, 1], w_dil_q_b[l - n_a], dil_q_tabs_p, q_rope_blocks,
                              group_dils=dils, group_blocks=GROUP_LANE_BLOCKS)
        res = [_dil_prompt_group(q_groups[gi], kv_groups[gi], dil, batch, seq) for gi, dil in enumerate(dils)]
        x = _dil_out([r[0] for r in res], [r[1] for r in res], w_dil_o_b[l - n_a], x, group_dils=dils)
        x = ffn(x, l, 1, final_norm=(l == depth - 1))
    y_prompt = x.reshape(batch, seq, d_model)
    mla_rows_prompt = jnp.stack(rows_p, axis=0).reshape(n_a, batch, seq, kv_lora + QK_ROPE)
    kv_p3 = kv_f32_p.reshape(batch, seq, N_GROUPS * kv_w)
    dil_p = []
    for gi, (win, _) in enumerate(DIL_GROUPS):
        keep = min(win, seq)
        tail = kv_p3[:, seq - keep:, gi * kv_w:(gi + 1) * kv_w]
        dil_p.append(tail.reshape(batch, keep, 2, HEADS_PER_GROUP, HEAD_DIM))

    dil_q_tabs_s = _dil_tables(pos_s, DIL_SCALE)
    dil_k_tabs_s = _dil_tables(pos_s, 1.0)
    bufs_t = []
    for buf in buffers:
        length = buf.shape[1]
        assert min(length, length + t_new) == length
        bufs_t.append(jnp.transpose(buf, (0, 2, 3, 4, 1)).reshape(bd, kv_w, length))
    new_rows_pad = SUBLANES
    x = xs
    kv_f32_s = _norm_rope(x, g_shared_kv, w_shared_b, dil_k_tabs_s, kv_rope_blocks)[0]
    kv_s = kv_f32_s.reshape(bd, t_new, N_GROUPS, kv_w)
    kv_new = jnp.pad(kv_s, ((0, 0), (0, new_rows_pad - t_new), (0, 0), (0, 0)))
    states_t = [None] * N_GROUPS
    for l in range(n_a, depth):
        x = ffn(x, l, 0)
        q = _norm_rope(x, g_layers[l, 1], w_dil_q_b[l - n_a], dil_q_tabs_s, q_rope_blocks)[0]
        qg = q.reshape(bd, t_new, N_GROUPS, GROUP_WIDTH)
        outs, lses = [], []
        for gi, dil in enumerate(dils):
            emit = l == n_a
            o, lse, st = _dil_sample_group(qg[:, :, gi], bufs_t[gi], kv_new[:, :, gi], dil, emit)
            if emit:
                states_t[gi] = st
            outs.append(o)
            lses.append(lse)
        x = _dil_out(outs, lses, w_dil_o_b[l - n_a], x)
        x = ffn(x, l, 1, final_norm=(l == depth - 1))
    y_sample = x.reshape(bd, t_new, d_model)
    mla_rows_sample = jnp.stack(rows_s, axis=0).reshape(n_a, bd, t_new, kv_lora + QK_ROPE)
    dil_s = []
    for gi, buf in enumerate(buffers):
        length = buf.shape[1]
        st = states_t[gi].reshape(bd, 2, HEADS_PER_GROUP, HEAD_DIM, length)
        dil_s.append(jnp.transpose(st, (0, 4, 1, 2, 3)))

    return (y_prompt, y_sample, mla_rows_prompt, mla_rows_sample,
            dil_p[0], dil_p[1], dil_p[2], dil_s[0], dil_s[1], dil_s[2])
```

```python
import functools
import math

import numpy as np
import jax
import jax.numpy as jnp
from jax import lax
from jax.experimental import pallas as pl
from jax.experimental.pallas import tpu as pltpu

F32 = jnp.float32
BF16 = jnp.bfloat16

MLA_HEADS = 16
QK_NOPE = 64
QK_ROPE = 32
V_DIM = 64
MLA_THETA = 10000.0
MLA_SCALE = (QK_NOPE + QK_ROPE) ** -0.5
DIL_GROUPS = ((128, 1), (512, 4), (2048, 16))
N_GROUPS = len(DIL_GROUPS)
HEADS_PER_GROUP = 8
HEAD_DIM = 64
GROUP_WIDTH = HEADS_PER_GROUP * HEAD_DIM
DIL_WIDTH = N_GROUPS * GROUP_WIDTH
ROT_DIM = HEAD_DIM // 4
ROPE_THETA = 500000.0
DIL_SCALE = HEAD_DIM ** -0.5
DIL_N = 128
EPS = 1e-6
NEG = -1e30
LOG2E = math.log2(math.e)

LANES = 128
SUBLANES = 8
HEAD_BLOCK = 128
GROUP_LANE_BLOCKS = GROUP_WIDTH // LANES
MXU_DIM = 256
FFN_SUB_COLS = 2 * MXU_DIM
VMEM_LIMIT_BYTES = 56 * 1024 * 1024
NT_DIMS = (((1,), (1,)), ((), ()))


def _cparams(*sem):
    return pltpu.CompilerParams(dimension_semantics=sem, vmem_limit_bytes=VMEM_LIMIT_BYTES)


def _row_tile(m, pref):
    t = min(m, pref)
    assert m % t == 0, (m, t)
    return t


def _rms(x, g):
    return x * lax.rsqrt(jnp.mean(x * x, axis=-1, keepdims=True) + EPS) * g


def _ffn_kernel(x_ref, g_ref, wg_ref, wu_ref, wo_ref, gf_ref, o_ref, *, final_norm, sub):
    x = x_ref[...]
    h = _rms(x, g_ref[...]).astype(BF16)
    d_ff = wg_ref.shape[1]
    part = None
    for lo in range(0, d_ff, sub):
        hi = min(lo + sub, d_ff)
        gate = jnp.dot(h, wg_ref[:, lo:hi], preferred_element_type=F32)
        up = jnp.dot(h, wu_ref[:, lo:hi], preferred_element_type=F32)
        act = (gate / (1.0 + jnp.exp(-gate)) * up).astype(BF16)
        contrib = jnp.dot(act, wo_ref[lo:hi, :], preferred_element_type=F32)
        part = contrib if part is None else part + contrib
    y = x + 0.5 * part
    if final_norm:
        y = _rms(y, gf_ref[...])
    o_ref[...] = y


def _ffn(x, g, w_in, w_out, layer, half, g_final, *, final_norm=False):
    m, d = x.shape
    d_ff = w_out.shape[2]
    tm = _row_tile(m, 512)
    kern = functools.partial(_ffn_kernel, final_norm=final_norm, sub=min(d_ff, FFN_SUB_COLS))
    resident = dict(pipeline_mode=pl.Buffered(1))
    return pl.pallas_call(
        kern,
        grid=(m // tm,),
        in_specs=[
            pl.BlockSpec((tm, d), lambda i: (i, 0)),
            pl.BlockSpec((1, d), lambda i: (0, 0)),
            pl.BlockSpec((None, None, d, d_ff), lambda i: (layer, half, 0, 0), **resident),
            pl.BlockSpec((None, None, d, d_ff), lambda i: (layer, half, 0, 1), **resident),
            pl.BlockSpec((None, None, d_ff, d), lambda i: (layer, half, 0, 0), **resident),
            pl.BlockSpec((1, d), lambda i: (0, 0)),
        ],
        out_specs=pl.BlockSpec((tm, d), lambda i: (i, 0)),
        out_shape=jax.ShapeDtypeStruct((m, d), F32),
        compiler_params=_cparams("parallel"),
        name="ffn",
    )(x, g.reshape(1, d), w_in, w_in, w_out, g_final.reshape(1, d))


def _linear_kernel(*refs, has_res):
    if has_res:
        a_ref, w_ref, r_ref, o_ref = refs
    else:
        a_ref, w_ref, o_ref = refs
    acc = jnp.dot(a_ref[...], w_ref[...], preferred_element_type=F32)
    if has_res:
        acc = acc + r_ref[...]
    o_ref[...] = acc.astype(o_ref.dtype)


def _linear(a, w, residual=None, out_dtype=F32):
    m, k = a.shape
    n = w.shape[1]
    tm = _row_tile(m, 512)
    tn = _row_tile(n, 1024)
    in_specs = [pl.BlockSpec((tm, k), lambda i, j: (i, 0)), pl.BlockSpec((k, tn), lambda i, j: (0, j))]
    args = [a, w]
    if residual is not None:
        in_specs.append(pl.BlockSpec((tm, tn), lambda i, j: (i, j)))
        args.append(residual)
    return pl.pallas_call(
        functools.partial(_linear_kernel, has_res=residual is not None),
        grid=(m // tm, n // tn),
        in_specs=in_specs,
        out_specs=pl.BlockSpec((tm, tn), lambda i, j: (i, j)),
        out_shape=jax.ShapeDtypeStruct((m, n), out_dtype),
        compiler_params=_cparams("parallel", "parallel"),
        name="linear",
    )(*args)


def _mla_proj_kernel(x_ref, g1_ref, win_ref, gq_ref, gkv_ref, wqb_ref, wukv_ref,
                     cq_ref, sq_ref, ck_ref, sk_ref, vone_ref,
                     q_ref, ckv_ref, kr_ref, *kv_refs, q_lora, kv_lora, with_kv):
    h = _rms(x_ref[...], g1_ref[...]).astype(BF16)
    comb = jnp.dot(h, win_ref[...], preferred_element_type=F32)
    cq = _rms(comb[:, :q_lora], gq_ref[...])
    ckv = _rms(comb[:, q_lora:q_lora + kv_lora], gkv_ref[...])
    blk = comb[:, q_lora + kv_lora:]
    kr = blk * ck_ref[...] + pltpu.roll(blk, HEAD_BLOCK - QK_ROPE, 1) * sk_ref[...]
    ckv_ref[...] = ckv
    kr_ref[...] = kr

    q = jnp.dot(cq.astype(BF16), wqb_ref[...], preferred_element_type=F32)
    cq_t = cq_ref[...]
    sq_t = sq_ref[...]
    for hd in range(MLA_HEADS):
        sl = slice(hd * HEAD_BLOCK, (hd + 1) * HEAD_BLOCK)
        qb = q[:, sl]
        q_ref[:, sl] = (qb * cq_t + pltpu.roll(qb, HEAD_BLOCK - QK_ROPE, 1) * sq_t).astype(BF16)

    if with_kv:
        k_ref, v_ref = kv_refs
        kv = jnp.dot(ckv.astype(BF16), wukv_ref[...], preferred_element_type=F32)
        width = MLA_HEADS * HEAD_BLOCK
        for hd in range(MLA_HEADS):
            sl = slice(hd * HEAD_BLOCK, (hd + 1) * HEAD_BLOCK)
            k_ref[:, sl] = (kv[:, sl] + kr).astype(BF16)
        v_ref[...] = (kv[:, width:] + vone_ref[...]).astype(BF16)


def _mla_proj(x, g1, w_in_ext, g_q, g_kv, w_qb_ext, w_ukv_ext, tabs, v_one, with_kv):
    m, d = x.shape
    q_lora = g_q.shape[0]
    kv_lora = g_kv.shape[0]
    tm = _row_tile(m, 512)
    width = MLA_HEADS * HEAD_BLOCK
    n_pos = tabs[0].shape[0] // tm
    full = lambda a: pl.BlockSpec(a.shape, lambda i: (0,) * a.ndim)
    tab_spec = pl.BlockSpec((tm, LANES), lambda i: (i % n_pos, 0))
    row = lambda n: pl.BlockSpec((tm, n), lambda i: (i, 0))
    out_shape = [jax.ShapeDtypeStruct((m, width), BF16), jax.ShapeDtypeStruct((m, kv_lora), F32),
                 jax.ShapeDtypeStruct((m, LANES), F32)]
    out_specs = [row(width), row(kv_lora), row(LANES)]
    if with_kv:
        out_shape += [jax.ShapeDtypeStruct((m, width), BF16), jax.ShapeDtypeStruct((m, width), BF16)]
        out_specs += [row(width), row(width)]
    g1 = g1.reshape(1, d)
    g_q = g_q.reshape(1, q_lora)
    g_kv = g_kv.reshape(1, kv_lora)
    return pl.pallas_call(
        functools.partial(_mla_proj_kernel, q_lora=q_lora, kv_lora=kv_lora, with_kv=with_kv),
        grid=(m // tm,),
        in_specs=[row(d), full(g1), full(w_in_ext), full(g_q), full(g_kv), full(w_qb_ext), full(w_ukv_ext),
                  tab_spec, tab_spec, tab_spec, tab_spec, full(v_one)],
        out_specs=out_specs,
        out_shape=out_shape,
        compiler_params=_cparams("parallel"),
        name="mla_proj",
    )(x, g1, w_in_ext, g_q, g_kv, w_qb_ext, w_ukv_ext, *tabs, v_one)


def _mla_flash_kernel(q_ref, k_ref, v_ref, o_ref, m_sc, acc_sc, sa_sc, sb_sc, *, tq):
    def q_block(qi, carry):
        _mla_flash_q_block(qi, q_ref, k_ref, v_ref, o_ref, m_sc, acc_sc, sa_sc, sb_sc, tq)
        return carry

    lax.fori_loop(0, q_ref.shape[0] // tq, q_block, 0)


def _mla_flash_q_block(qi, q_ref, k_ref, v_ref, o_ref, m_sc, acc_sc, sa_sc, sb_sc, tq):
    tk = tq
    q_start = pl.multiple_of(qi * tq, tq)
    heads = (slice(0, HEAD_BLOCK), slice(HEAD_BLOCK, 2 * HEAD_BLOCK))
    m_sc[...] = jnp.full(m_sc.shape, NEG, F32)
    acc_sc[...] = jnp.zeros(acc_sc.shape, F32)

    def scores(kc, hh):
        start = pl.multiple_of(kc * tk, tk)
        return lax.dot_general(q_ref[pl.ds(q_start, tq), heads[hh]], k_ref[pl.ds(start, tk), heads[hh]], NT_DIMS,
                               preferred_element_type=F32)

    def softmax_pv(kc, hh, s, masked):
        start = pl.multiple_of(kc * tk, tk)
        if masked:
            row = lax.broadcasted_iota(jnp.int32, (tq, tk), 0)
            col = lax.broadcasted_iota(jnp.int32, (tq, tk), 1)
            s = jnp.where(col <= row, s, NEG)
        m_prev = m_sc[hh]
        m_new = jnp.maximum(m_prev, jnp.max(s, axis=-1, keepdims=True))
        alpha = jnp.exp2(m_prev - m_new)
        p = jnp.exp2(s - jnp.concatenate([m_new] * (tk // LANES), axis=1))
        v = v_ref[pl.ds(start, tk), heads[hh]]
        acc_sc[hh] = acc_sc[hh] * alpha + jnp.dot(p.astype(BF16), v, preferred_element_type=F32)
        m_sc[hh] = m_new

    def chunk(kc, cur, nxt, masked):
        if nxt is not None:
            for hh in range(2):
                nxt[hh] = scores(kc + 1, hh)
        for hh in range(2):
            softmax_pv(kc, hh, cur[hh], masked)

    for hh in range(2):
        sa_sc[hh] = scores(0, hh)

    def run_pairs(first, n_pairs):
        for j in range(n_pairs):
            chunk(first + 2 * j, sa_sc, sb_sc, False)
            chunk(first + 2 * j + 1, sb_sc, sa_sc, False)

    def body(octet, carry):
        run_pairs(8 * octet, 4)
        return carry

    n_octets = lax.shift_right_logical(qi, 3)
    lax.fori_loop(0, n_octets, body, 0)

    @pl.when((qi & 4) != 0)
    def _():
        run_pairs(8 * n_octets, 2)

    @pl.when((qi & 2) != 0)
    def _():
        run_pairs(8 * n_octets + (qi & 4), 1)

    odd = (qi & 1) == 1

    @pl.when(odd)
    def _():
        chunk(qi - 1, sa_sc, sb_sc, False)
        chunk(qi, sb_sc, None, True)

    @pl.when(jnp.logical_not(odd))
    def _():
        chunk(qi, sa_sc, None, True)

    lane = lax.broadcasted_iota(jnp.int32, (tq, HEAD_BLOCK), 1)
    acc0 = acc_sc[0]
    acc1 = acc_sc[1]
    out0 = acc0 / acc0[:, V_DIM:V_DIM + 1]
    out1 = acc1 / acc1[:, 0:1]
    o_ref[pl.ds(q_start, tq), :] = jnp.where(lane < V_DIM, out0, out1).astype(o_ref.dtype)


def _mla_flash(q, k, v, batch, seq):
    width = MLA_HEADS * HEAD_BLOCK
    q = q.reshape(batch, seq, width)
    k = k.reshape(batch, seq, width)
    v = v.reshape(batch, seq, width)
    tq = _row_tile(seq, 512)
    pair = 2 * HEAD_BLOCK
    out = pl.pallas_call(
        functools.partial(_mla_flash_kernel, tq=tq),
        grid=(batch, MLA_HEADS // 2),
        in_specs=[pl.BlockSpec((None, seq, pair), lambda b, h: (b, 0, h))] * 3,
        out_specs=pl.BlockSpec((None, seq, 2 * V_DIM), lambda b, h: (b, 0, h)),
        out_shape=jax.ShapeDtypeStruct((batch, seq, MLA_HEADS * V_DIM), BF16),
        scratch_shapes=[pltpu.VMEM((2, tq, HEAD_BLOCK), F32), pltpu.VMEM((2, tq, HEAD_BLOCK), F32),
                        pltpu.VMEM((2, tq, tq), F32), pltpu.VMEM((2, tq, tq), F32)],
        compiler_params=_cparams("parallel", "parallel"),
        name="mla_flash",
    )(q, k, v)
    return out.reshape(batch * seq, MLA_HEADS * V_DIM)


def _mla_decode_kernel(pt_ref, ql_ref, qr_ref, newt_ref, *rest, chain_tiles, page, kv_lora, t_new):
    n_pages = sum(chain_tiles) - 1
    _mla_decode_compute(ql_ref, qr_ref, list(rest[:n_pages]) + [newt_ref], rest[n_pages], rest[n_pages + 1:],
                        chain_tiles, page, kv_lora, t_new)


def _mla_decode_compute(ql_ref, qr_ref, tiles, o_ref, scratch, chain_tiles, page, kv_lora, t_new):
    rows = ql_ref.shape[0]
    ql = ql_ref[...]
    qr = qr_ref[...]
    stats = []
    first = 0
    for g, n_tiles in enumerate(chain_tiles):
        lat_sc, kr_sc = scratch[2 * g], scratch[2 * g + 1]
        for i, ref in enumerate(tiles[first:first + n_tiles]):
            lat_sc[:, i * page:(i + 1) * page] = ref[:kv_lora, :].astype(BF16)
            kr_sc[:, i * page:(i + 1) * page] = ref[kv_lora:, :].astype(BF16)
        first += n_tiles
        s = (jnp.dot(ql, lat_sc[...], preferred_element_type=F32)
             + jnp.dot(qr, kr_sc[...], preferred_element_type=F32))
        if g == len(chain_tiles) - 1:
            n_keys = n_tiles * page
            col = lax.broadcasted_iota(jnp.int32, (rows, n_keys), 1)
            t_row = lax.shift_right_logical(lax.broadcasted_iota(jnp.int32, (rows, n_keys), 0),
                                            int(math.log2(MLA_HEADS)))
            j_new = col - (n_tiles - 1) * page
            s = jnp.where((j_new < 0) | ((j_new <= t_row) & (j_new < t_new)), s, NEG)
        m = jnp.max(s, axis=-1, keepdims=True)
        p = jnp.exp(s - m)
        l = jnp.sum(p, axis=-1, keepdims=True)
        pv = lax.dot_general(p.astype(BF16), lat_sc[...], NT_DIMS, preferred_element_type=F32)
        stats.append((m, l, pv))
    m_all = functools.reduce(jnp.maximum, [st[0] for st in stats])
    num = den = None
    for m, l, pv in stats:
        a = jnp.exp(m - m_all)
        num = a * pv if num is None else num + a * pv
        den = a * l if den is None else den + a * l
    o_ref[...] = (num / den).astype(o_ref.dtype)


def _mla_decode(q_lat, q_rope, new_t, cache_t, layer, page_table):
    bd, rows, kv_lora = q_lat.shape
    n_pages = page_table.shape[1]
    row_w, page = cache_t.shape[2], cache_t.shape[3]
    t_new = rows // MLA_HEADS
    chain_tiles = _decode_chains(n_pages)

    def page_spec(i):
        return pl.BlockSpec((None, None, row_w, page), lambda b, pt: (layer, pt[b, i], 0, 0))

    scratch = []
    for n_tiles in chain_tiles:
        scratch += [pltpu.VMEM((kv_lora, n_tiles * page), BF16), pltpu.VMEM((QK_ROPE, n_tiles * page), BF16)]
    grid_spec = pltpu.PrefetchScalarGridSpec(
        num_scalar_prefetch=1,
        grid=(bd,),
        in_specs=[
            pl.BlockSpec((None, rows, kv_lora), lambda b, pt: (b, 0, 0)),
            pl.BlockSpec((None, rows, QK_ROPE), lambda b, pt: (b, 0, 0)),
            pl.BlockSpec((None, row_w, page), lambda b, pt: (b, 0, 0)),
        ] + [page_spec(i) for i in range(n_pages)],
        out_specs=pl.BlockSpec((None, rows, kv_lora), lambda b, pt: (b, 0, 0)),
        scratch_shapes=scratch,
    )
    kern = functools.partial(_mla_decode_kernel, chain_tiles=chain_tiles, page=page, kv_lora=kv_lora, t_new=t_new)
    return pl.pallas_call(
        kern,
        grid_spec=grid_spec,
        out_shape=jax.ShapeDtypeStruct((bd, rows, kv_lora), BF16),
        compiler_params=_cparams("parallel"),
        name="mla_decode",
    )(page_table, q_lat, q_rope, new_t, *([cache_t] * n_pages))


def _decode_chains(n_pages):
    n_chains = 4 if n_pages >= 8 else 1
    base, extra = divmod(n_pages + 1, n_chains)
    return tuple(base + (1 if g >= n_chains - extra else 0) for g in range(n_chains))


def _norm_rope_kernel(x_ref, g_ref, w_ref, c_ref, s1_ref, s2_ref, *refs, rope_blocks, group_dils, group_blocks,
                      n_f32_out):
    n_groups = 0 if group_dils is None else len(group_dils)
    n_out = (n_groups if group_dils is not None else 0) + n_f32_out
    out_refs = refs[:n_out]
    sc = refs[n_out] if group_dils is not None else None
    tm = x_ref.shape[0]
    h = _rms(x_ref[...], g_ref[...]).astype(BF16)
    y = jnp.dot(h, w_ref[...], preferred_element_type=F32)
    ct, s1, s2 = c_ref[...], s1_ref[...], s2_ref[...]
    half = ROT_DIM // 2
    for j, (roped, dest) in enumerate(rope_blocks):
        blk = y[:, j * LANES:(j + 1) * LANES]
        if roped:
            blk = blk * ct + pltpu.roll(blk, LANES - half, 1) * s1 + pltpu.roll(blk, half, 1) * s2
        if n_f32_out:
            out_refs[n_out - 1][:, dest * LANES:(dest + 1) * LANES] = blk
        if group_dils is not None:
            gi, jj = divmod(dest, group_blocks)
            dil = group_dils[gi]
            width = group_blocks * LANES
            if dil == 1:
                out_refs[gi][:, jj * LANES:(jj + 1) * LANES] = blk.astype(BF16)
            else:
                sc[...] = blk
                for r in range(dil):
                    lo = r * width + jj * LANES
                    out_refs[gi][:, lo:lo + LANES] = sc[pl.ds(r, tm // dil, stride=dil), :].astype(BF16)


def _norm_rope(x, g, w, tabs, rope_blocks, *, group_dils=None, group_blocks=None, emit_f32=False):
    m, d = x.shape
    n = w.shape[1]
    tm = _row_tile(m, 512)
    n_pos = tabs[0].shape[0] // tm
    tab_spec = pl.BlockSpec((tm, LANES), lambda i: (i % n_pos, 0))
    out_shape, out_specs, scratch = [], [], []
    if group_dils is not None:
        width = group_blocks * LANES
        for dil in group_dils:
            assert tm % (dil * 2 * SUBLANES) == 0
            out_shape.append(jax.ShapeDtypeStruct((m // dil, dil * width), BF16))
            out_specs.append(pl.BlockSpec((tm // dil, dil * width), lambda i: (i, 0)))
        scratch = [pltpu.VMEM((tm, LANES), F32)]
    else:
        emit_f32 = True
    if emit_f32:
        out_shape.append(jax.ShapeDtypeStruct((m, n), F32))
        out_specs.append(pl.BlockSpec((tm, n), lambda i: (i, 0)))
    return pl.pallas_call(
        functools.partial(_norm_rope_kernel, rope_blocks=tuple(rope_blocks), group_dils=group_dils,
                          group_blocks=group_blocks, n_f32_out=int(emit_f32)),
        grid=(m // tm,),
        in_specs=[pl.BlockSpec((tm, d), lambda i: (i, 0)), pl.BlockSpec((1, d), lambda i: (0, 0)),
                  pl.BlockSpec((d, n), lambda i: (0, 0)), tab_spec, tab_spec, tab_spec],
        out_specs=out_specs,
        out_shape=out_shape,
        scratch_shapes=scratch,
        compiler_params=_cparams("parallel"),
        name="norm_rope",
    )(x, g.reshape(1, d), w, *tabs)


def _dil_prompt_kernel(q_ref, kp_ref, kc_ref, vp_ref, vc_ref, o_ref, lse_ref, *, tq, n_sub):
    cblk = pl.program_id(2)
    lane = lax.broadcasted_iota(jnp.int32, (tq, LANES), 1)
    i_idx = lax.broadcasted_iota(jnp.int32, (2 * tq, 2 * tq), 0) & (tq - 1)
    j_idx = lax.broadcasted_iota(jnp.int32, (2 * tq, 2 * tq), 1)
    valid_cur = (j_idx >= tq) & (j_idx - tq <= i_idx)
    valid_mid = ((j_idx < tq) & (j_idx >= i_idx)) | valid_cur
    prev_off = jnp.where(cblk > 0, 0, tq)
    valid_first = ((j_idx < tq) & (j_idx >= i_idx + prev_off)) | valid_cur
    for sub in range(n_sub):
        rs = slice(sub * tq, (sub + 1) * tq)
        ps = slice((sub - 1) * tq, sub * tq)
        valid = valid_first if sub == 0 else valid_mid
        for hp in range(HEADS_PER_GROUP // 2):
            sl = slice(hp * LANES, (hp + 1) * LANES)
            q2 = q_ref[rs, sl]
            zero = jnp.zeros_like(q2)
            qs = jnp.concatenate([jnp.where(lane < HEAD_DIM, q2, zero), jnp.where(lane >= HEAD_DIM, q2, zero)],
                                 axis=0)
            k_prev = kp_ref[:, sl] if sub == 0 else kc_ref[ps, sl]
            v_prev = vp_ref[:, sl] if sub == 0 else vc_ref[ps, sl]
            kcat = jnp.concatenate([k_prev, kc_ref[rs, sl]], axis=0)
            vcat = jnp.concatenate([v_prev, vc_ref[rs, sl]], axis=0)
            s = jnp.where(valid, lax.dot_general(qs, kcat, NT_DIMS, preferred_element_type=F32), NEG)
            m = jnp.max(s, axis=-1, keepdims=True)
            p = jnp.exp(s - m)
            den = jnp.sum(p, axis=-1, keepdims=True)
            o2 = jnp.dot(p.astype(BF16), vcat, preferred_element_type=F32) / den
            lse2 = jnp.broadcast_to(m + jnp.log(den), (2 * tq, LANES))
            o_ref[rs, sl] = jnp.where(lane < HEAD_DIM, o2[:tq], o2[tq:]).astype(o_ref.dtype)
            lse_ref[rs, sl] = jnp.where(lane < HEAD_DIM, lse2[:tq], lse2[tq:])


def _dil_prompt_group(q_g, kv_g, dil, batch, seq):
    assert seq % (dil * DIL_N) == 0
    rows = seq // dil
    tq = DIL_N
    n_sub = min(4, rows // tq)
    nblk = rows // (tq * n_sub)
    qv = q_g.reshape(batch, rows, dil * GROUP_WIDTH)
    kvv = kv_g.reshape(batch, rows, dil * 2 * GROUP_WIDTH)
    blk = (None, tq * n_sub, GROUP_WIDTH)
    pblk = (None, tq, GROUP_WIDTH)
    cur = lambda off: pl.BlockSpec(blk, lambda b, r, c: (b, c, 2 * r + off))
    prev = lambda off: pl.BlockSpec(pblk, lambda b, r, c: (b, jnp.maximum(c * n_sub - 1, 0), 2 * r + off))
    rspec = pl.BlockSpec(blk, lambda b, r, c: (b, c, r))
    o, lse = pl.pallas_call(
        functools.partial(_dil_prompt_kernel, tq=tq, n_sub=n_sub),
        grid=(batch, dil, nblk),
        in_specs=[rspec, prev(0), cur(0), prev(1), cur(1)],
        out_specs=[rspec, rspec],
        out_shape=[jax.ShapeDtypeStruct((batch, rows, dil * GROUP_WIDTH), BF16),
                   jax.ShapeDtypeStruct((batch, rows, dil * GROUP_WIDTH), F32)],
        compiler_params=_cparams("parallel", "parallel", "arbitrary"),
        name="dil_prompt",
    )(qv, kvv, kvv, kvv, kvv)
    return o.reshape(batch * rows, dil * GROUP_WIDTH), lse.reshape(batch * rows, dil * GROUP_WIDTH)


def _dil_sample_kernel(q_ref, buf_ref, new_ref, *rest, bb, dil, t_new, emit_state):
    if emit_state:
        o_ref, lse_ref, state_ref = rest
    else:
        o_ref, lse_ref = rest
    w = buf_ref.shape[2]
    rows = t_new * HEADS_PER_GROUP
    log_h = int(math.log2(HEADS_PER_GROUP))
    lane = lax.broadcasted_iota(jnp.int32, (rows, GROUP_WIDTH), 1)
    row = lax.broadcasted_iota(jnp.int32, (rows, GROUP_WIDTH), 0)
    diag = lax.shift_right_logical(lane, int(math.log2(HEAD_DIM))) == (row & (HEADS_PER_GROUP - 1))
    w_idx = lax.broadcasted_iota(jnp.int32, (rows, w), 1)
    t_of_row = lax.shift_right_logical(lax.broadcasted_iota(jnp.int32, (rows, w), 0), log_h)
    valid = (w_idx >= t_of_row) if dil == 1 else ((w_idx & (dil - 1)) == t_of_row)
    n_new = new_ref.shape[1]
    j_new = lax.broadcasted_iota(jnp.int32, (rows, n_new), 1)
    t_new_row = lax.shift_right_logical(lax.broadcasted_iota(jnp.int32, (rows, n_new), 0), log_h)
    valid_new = (j_new <= t_new_row) if dil == 1 else (j_new == t_new_row)
    shift_lane = lax.broadcasted_iota(jnp.int32, (2 * GROUP_WIDTH, LANES), 1)
    for b in range(bb):
        q = q_ref[b]
        qrep = jnp.concatenate(
            [jnp.broadcast_to(q[t:t + 1, :], (HEADS_PER_GROUP, GROUP_WIDTH)) for t in range(t_new)], axis=0)
        qbd = jnp.where(diag, qrep, 0.0).astype(BF16)
        kt = buf_ref[b, :GROUP_WIDTH, :].astype(BF16)
        vt = buf_ref[b, GROUP_WIDTH:, :].astype(BF16)
        k_new = new_ref[b, :, :GROUP_WIDTH].astype(BF16)
        v_new = new_ref[b, :, GROUP_WIDTH:].astype(BF16)
        s = jnp.where(valid, jnp.dot(qbd, kt, preferred_element_type=F32), NEG)
        s_new = jnp.where(valid_new, lax.dot_general(qbd, k_new, NT_DIMS, preferred_element_type=F32), NEG)
        m = jnp.maximum(jnp.max(s, axis=-1, keepdims=True), jnp.max(s_new, axis=-1, keepdims=True))
        p = jnp.exp(s - m)
        p_new = jnp.exp(s_new - m)
        den = jnp.sum(p, axis=-1, keepdims=True) + jnp.sum(p_new, axis=-1, keepdims=True)
        o = (lax.dot_general(p.astype(BF16), vt, NT_DIMS, preferred_element_type=F32)
             + jnp.dot(p_new.astype(BF16), v_new, preferred_element_type=F32)) / den
        lse = jnp.broadcast_to(m + jnp.log(den), (rows, GROUP_WIDTH))
        o_ref[b] = jnp.sum(jnp.where(diag, o, 0.0).reshape(t_new, HEADS_PER_GROUP, GROUP_WIDTH), axis=1)
        lse_ref[b] = jnp.sum(jnp.where(diag, lse, 0.0).reshape(t_new, HEADS_PER_GROUP, GROUP_WIDTH), axis=1)
        if emit_state:
            n_col = w // LANES
            nxt = pltpu.roll(buf_ref[b, :, 0:LANES], LANES - t_new, 1)
            for jc in range(n_col):
                cur = nxt
                if jc + 1 < n_col:
                    nxt = pltpu.roll(buf_ref[b, :, (jc + 1) * LANES:(jc + 2) * LANES], LANES - t_new, 1)
                    cur = jnp.where(shift_lane < LANES - t_new, cur, nxt)
                state_ref[b, :, jc * LANES:(jc + 1) * LANES] = cur
            new_rows = jnp.concatenate([new_ref[b], jnp.zeros((LANES - n_new, 2 * GROUP_WIDTH), F32)], axis=0)
            state_ref[b, :, w - t_new:w] = new_rows.T[:, :t_new]


def _dil_sample_group(q_g, buf_t, new_g, dil, emit_state):
    bd, t_new, _ = q_g.shape
    w = buf_t.shape[2]
    assert w == dil * DIL_N, "state buffer must hold exactly one window"
    assert dil == 1 or dil >= t_new
    kv_w = 2 * GROUP_WIDTH
    bb = _row_tile(bd, max(1, 1024 // w))
    spec3 = lambda a: pl.BlockSpec((bb,) + a.shape[1:], lambda i: (i, 0, 0))
    in_specs = [spec3(q_g), spec3(buf_t), spec3(new_g)]
    args = [q_g, buf_t, new_g]
    out_spec = pl.BlockSpec((bb, t_new, GROUP_WIDTH), lambda i: (i, 0, 0))
    out_specs = [out_spec, out_spec]
    out_shape = [jax.ShapeDtypeStruct((bd, t_new, GROUP_WIDTH), F32)] * 2
    if emit_state:
        out_specs.append(spec3(buf_t))
        out_shape.append(jax.ShapeDtypeStruct(buf_t.shape, F32))
    res = pl.pallas_call(
        functools.partial(_dil_sample_kernel, bb=bb, dil=dil, t_new=t_new, emit_state=emit_state),
        grid=(bd // bb,),
        in_specs=in_specs,
        out_specs=out_specs,
        out_shape=out_shape,
        compiler_params=_cparams("parallel"),
        name="dil_sample",
    )(*args)
    o, lse = res[0].reshape(bd * t_new, GROUP_WIDTH), res[1].reshape(bd * t_new, GROUP_WIDTH)
    return o, lse, (res[2] if emit_state else None)


def _dil_out_kernel(*refs, group_dils):
    n_g = N_GROUPS
    o_refs, l_refs = refs[:n_g], refs[n_g:2 * n_g]
    w_ref, x_ref, y_ref = refs[2 * n_g:2 * n_g + 3]
    scratch = refs[2 * n_g + 3:]
    tm = x_ref.shape[0]
    cols = []
    for jj in range(GROUP_LANE_BLOCKS):
        sl = slice(jj * LANES, (jj + 1) * LANES)
        os_, ls_ = [], []
        for gi in range(n_g):
            dil = 1 if group_dils is None else group_dils[gi]
            if dil == 1:
                os_.append(o_refs[gi][:, sl].astype(F32))
                ls_.append(l_refs[gi][:, sl])
            else:
                for src, dst in ((o_refs[gi], scratch[0]), (l_refs[gi], scratch[1])):
                    for r in range(dil):
                        lo = r * GROUP_WIDTH + jj * LANES
                        dst[pl.ds(r, tm // dil, stride=dil), :] = src[:, lo:lo + LANES].astype(F32)
                os_.append(scratch[0][...])
                ls_.append(scratch[1][...])
        m = jnp.maximum(jnp.maximum(ls_[0], ls_[1]), ls_[2])
        e = [jnp.exp(l - m) for l in ls_]
        den = e[0] + e[1] + e[2]
        cols.append(((e[0] / den) * os_[0] + (e[1] / den) * os_[1] + (e[2] / den) * os_[2]).astype(BF16))
    o = jnp.concatenate(cols, axis=1)
    y_ref[...] = x_ref[...] + jnp.dot(o, w_ref[...], preferred_element_type=F32)


def _dil_out(outs, lses, w_o, x, group_dils=None):
    m, d = x.shape
    tm = _row_tile(m, 512)
    specs = []
    for gi in range(N_GROUPS):
        dil = 1 if group_dils is None else group_dils[gi]
        specs.append(pl.BlockSpec((tm // dil, dil * GROUP_WIDTH), lambda i: (i, 0)))
    scratch = [] if group_dils is None else [pltpu.VMEM((tm, LANES), F32)] * 2
    return pl.pallas_call(
        functools.partial(_dil_out_kernel, group_dils=group_dils),
        grid=(m // tm,),
        in_specs=specs + specs + [pl.BlockSpec(w_o.shape, lambda i: (0, 0)), pl.BlockSpec((tm, d), lambda i: (i, 0))],
        out_specs=pl.BlockSpec((tm, d), lambda i: (i, 0)),
        out_shape=jax.ShapeDtypeStruct((m, d), F32),
        scratch_shapes=scratch,
        compiler_params=_cparams("parallel"),
        name="dil_out",
    )(*outs, *lses, w_o, x)


def _rope_cos_sin(pos, rot_dim, theta):
    half = rot_dim // 2
    inv = jnp.float32(theta) ** (-2.0 * jnp.arange(half, dtype=F32) / rot_dim)
    ang = pos[:, None] * inv[None, :]
    return jnp.cos(ang), jnp.sin(ang)


def _mla_tables(pos, q_scale):
    cos, sin = _rope_cos_sin(pos, QK_ROPE, MLA_THETA)
    n = pos.shape[0]
    z = lambda w: jnp.zeros((n, w), F32)
    cos2 = jnp.concatenate([cos, cos], axis=1)
    sin2 = jnp.concatenate([sin, sin], axis=1)
    rest = HEAD_BLOCK - QK_NOPE - QK_ROPE
    cq = jnp.concatenate([jnp.ones((n, QK_NOPE), F32), cos2, z(rest)], axis=1) * q_scale
    sq = jnp.concatenate([z(QK_NOPE), sin2, z(rest)], axis=1) * q_scale
    ck = jnp.concatenate([z(QK_NOPE), cos2, z(rest)], axis=1)
    sk = jnp.concatenate([z(QK_NOPE), sin2, z(rest)], axis=1)
    return cq, sq, ck, sk


def _dil_tables(pos, scale):
    cos, sin = _rope_cos_sin(pos, ROT_DIM, ROPE_THETA)
    n = pos.shape[0]
    half = ROT_DIM // 2
    z = lambda w: jnp.zeros((n, w), F32)
    keep = jnp.ones((n, HEAD_DIM - ROT_DIM), F32)
    c_head = jnp.concatenate([cos, cos, keep], axis=1)
    s1_head = jnp.concatenate([-sin, z(HEAD_DIM - half)], axis=1)
    s2_head = jnp.concatenate([z(half), sin, z(HEAD_DIM - ROT_DIM)], axis=1)
    two = lambda a: jnp.concatenate([a, a], axis=1) * scale
    return two(c_head), two(s1_head), two(s2_head)


def _rot_half_cols(w):
    half = w.shape[-1] // 2
    return jnp.concatenate([-w[..., half:], w[..., :half]], axis=-1)


def _prep_mla_weights(w_in, w_qb, w_ukv, q_lora, kv_lora):
    d = w_in.shape[0]
    w_kr = w_in[:, q_lora + kv_lora:]
    assert HEAD_BLOCK - QK_NOPE - QK_ROPE == QK_ROPE
    w_in_ext = jnp.concatenate(
        [w_in[:, :q_lora + kv_lora], jnp.zeros((d, QK_NOPE), F32), w_kr, _rot_half_cols(w_kr)], axis=1)
    wq = w_qb.reshape(q_lora, MLA_HEADS, QK_NOPE + QK_ROPE)
    wq_rope = wq[..., QK_NOPE:]
    w_qb_ext = jnp.concatenate([wq[..., :QK_NOPE], wq_rope, _rot_half_cols(wq_rope)], axis=-1)
    w_qb_ext = w_qb_ext.reshape(q_lora, MLA_HEADS * HEAD_BLOCK)
    wkv = w_ukv.reshape(kv_lora, MLA_HEADS, QK_NOPE + V_DIM)
    zpad = jnp.zeros((kv_lora, MLA_HEADS, HEAD_BLOCK - QK_NOPE), F32)
    wk = jnp.concatenate([wkv[..., :QK_NOPE], zpad], axis=-1).reshape(kv_lora, MLA_HEADS * HEAD_BLOCK)
    wv_even = jnp.concatenate([wkv[..., QK_NOPE:], zpad], axis=-1)
    wv_odd = jnp.concatenate([zpad, wkv[..., QK_NOPE:]], axis=-1)
    odd = (jnp.arange(MLA_HEADS) % 2 == 1)[None, :, None]
    wv = jnp.where(odd, wv_odd, wv_even).reshape(kv_lora, MLA_HEADS * HEAD_BLOCK)
    w_ukv_ext = jnp.concatenate([wk, wv], axis=1)
    one = np.zeros((1, MLA_HEADS, HEAD_BLOCK), np.float32)
    one[0, 0::2, V_DIM] = 1.0
    one[0, 1::2, 0] = 1.0
    v_one = jnp.asarray(one.reshape(1, MLA_HEADS * HEAD_BLOCK))
    eye = jnp.eye(MLA_HEADS, dtype=F32)
    w_uk = wkv[..., :QK_NOPE]
    w_uk_rows = jnp.concatenate([jnp.transpose(w_uk, (1, 2, 0)),
                                 jnp.zeros((MLA_HEADS, HEAD_BLOCK - QK_NOPE, kv_lora), F32)], axis=1)
    w_uk_bd = jnp.einsum('hnc,hg->hngc', w_uk_rows, eye).reshape(MLA_HEADS * HEAD_BLOCK, MLA_HEADS * kv_lora)
    w_uv = jnp.transpose(wkv[..., QK_NOPE:], (1, 0, 2))
    w_uv_bd = jnp.einsum('hcv,hg->hcgv', w_uv, eye).reshape(MLA_HEADS * kv_lora, MLA_HEADS * V_DIM)
    bf = lambda a: a.astype(BF16)
    return bf(w_in_ext), bf(w_qb_ext), bf(w_ukv_ext), v_one, bf(w_uk_bd), bf(w_uv_bd)


def kernel(x_prompt, x_sample, cache_mla, page_table, state_dil_w128, state_dil_w512, state_dil_w2048,
           g_layers, w_ffn_in, w_ffn_out, w_mla_in, g_mla_q, g_mla_kv, w_mla_qb, w_mla_ukv, w_mla_o,
           g_shared_kv, w_shared_kv, w_dil_q, w_dil_o, g_final):
    batch, seq, d_model = x_prompt.shape
    bd, t_new, _ = x_sample.shape
    depth = g_layers.shape[0]
    n_a = w_mla_in.shape[0]
    q_lora = g_mla_q.shape[1]
    kv_lora = g_mla_kv.shape[1]
    page = cache_mla.shape[2]
    past_len = page_table.shape[1] * page
    buffers = [state_dil_w128, state_dil_w512, state_dil_w2048]
    dils = tuple(dil for _, dil in DIL_GROUPS)
    kv_w = 2 * GROUP_WIDTH

    w_ffn_in_b = w_ffn_in.astype(BF16)
    w_ffn_out_b = w_ffn_out.astype(BF16)
    mla_w = [_prep_mla_weights(w_mla_in[a], w_mla_qb[a], w_mla_ukv[a], q_lora, kv_lora) for a in range(n_a)]
    w_mla_o_b = w_mla_o.astype(BF16)
    w_shared_b = w_shared_kv.astype(BF16)
    w_dil_q_b = w_dil_q.astype(BF16)
    w_dil_o_b = w_dil_o.astype(BF16)
    kv_rope_blocks = []
    for is_v in range(2):
        for gi in range(N_GROUPS):
            for jj in range(GROUP_LANE_BLOCKS):
                kv_rope_blocks.append((not is_v, (2 * gi + is_v) * GROUP_LANE_BLOCKS + jj))
    q_rope_blocks = [(True, j) for j in range(DIL_WIDTH // LANES)]

    pos_p = jnp.arange(seq, dtype=F32)
    pos_s = jnp.tile((past_len + jnp.arange(t_new)).astype(F32), bd)

    def ffn(x, l, half, final_norm=False):
        return _ffn(x, g_layers[l, 2 * half], w_ffn_in_b, w_ffn_out_b, l, half, g_final, final_norm=final_norm)

    mla_tabs_p = _mla_tables(pos_p, MLA_SCALE * LOG2E)
    dil_q_tabs_p = _dil_tables(pos_p, DIL_SCALE)
    dil_k_tabs_p = _dil_tables(pos_p, 1.0)
    mla_tabs_s = _mla_tables(pos_s, MLA_SCALE)
    cache_t = jnp.swapaxes(cache_mla, 2, 3)
    xs = x_sample.reshape(bd * t_new, d_model)
    rows_s = []

    x = x_prompt.reshape(batch * seq, d_model)
    rows_p = []
    kv_groups = kv_f32_p = None
    for l in range(depth):
        if l < n_a:
            w_in_ext, w_qb_ext, w_ukv_ext, v_one, w_uk_bd, w_uv_bd = mla_w[l]
            xs = ffn(xs, l, 0)
            q_s, ckv_s, kr_s = _mla_proj(xs, g_layers[l, 1], w_in_ext, g_mla_q[l], g_mla_kv[l], w_qb_ext,
                                         w_ukv_ext, mla_tabs_s, v_one, with_kv=False)
            rows = jnp.concatenate([ckv_s, kr_s[:, QK_NOPE:QK_NOPE + QK_ROPE]], axis=1)
            rows_s.append(rows)
            q_lat = _linear(q_s, w_uk_bd, out_dtype=BF16).reshape(bd, t_new * MLA_HEADS, kv_lora)
            q_rope = q_s.reshape(bd, t_new * MLA_HEADS, HEAD_BLOCK)[:, :, QK_NOPE:QK_NOPE + QK_ROPE]
            new_pad = jnp.pad(rows.reshape(bd, t_new, kv_lora + QK_ROPE), ((0, 0), (0, page - t_new), (0, 0)))
            new_t = jnp.swapaxes(new_pad, 1, 2)
            o_lat = _mla_decode(q_lat, q_rope, new_t, cache_t, l, page_table)
            x = ffn(x, l, 0)
            q, ckv, kr, k, v = _mla_proj(x, g_layers[l, 1], w_in_ext, g_mla_q[l], g_mla_kv[l], w_qb_ext, w_ukv_ext,
                                         mla_tabs_p, v_one, with_kv=True)
            o = _mla_flash(q, k, v, batch, seq)
            x = _linear(o, w_mla_o_b[l], residual=x)
            rows_p.append(jnp.concatenate([ckv, kr[:, QK_NOPE:QK_NOPE + QK_ROPE]], axis=1))
            x = ffn(x, l, 1)
            o_s = _linear(o_lat.reshape(bd * t_new, MLA_HEADS * kv_lora), w_uv_bd, out_dtype=BF16)
            xs = _linear(o_s, w_mla_o_b[l], residual=xs)
            xs = ffn(xs, l, 1)
            if l == n_a - 1:
                *kv_groups, kv_f32_p = _norm_rope(x, g_shared_kv, w_shared_b, dil_k_tabs_p, kv_rope_blocks,
                                                  group_dils=dils, group_blocks=2 * GROUP_LANE_BLOCKS, emit_f32=True)
            continue
        x = ffn(x, l, 0)
        q_groups = _norm_rope(x, g_layers[l, 1], w_dil_q_b[l - n_a], dil_q_tabs_p, q_rope_blocks,
                              group_dils=dils, group_blocks=GROUP_LANE_BLOCKS)
        res = [_dil_prompt_group(q_groups[gi], kv_groups[gi], dil, batch, seq) for gi, dil in enumerate(dils)]
        x = _dil_out([r[0] for r in res], [r[1] for r in res], w_dil_o_b[l - n_a], x, group_dils=dils)
        x = ffn(x, l, 1, final_norm=(l == depth - 1))
    y_prompt = x.reshape(batch, seq, d_model)
    mla_rows_prompt = jnp.stack(rows_p, axis=0).reshape(n_a, batch, seq, kv_lora + QK_ROPE)
    kv_p3 = kv_f32_p.reshape(batch, seq, N_GROUPS * kv_w)
    dil_p = []
    for gi, (win, _) in enumerate(DIL_GROUPS):
        keep = min(win, seq)
        tail = kv_p3[:, seq - keep:, gi * kv_w:(gi + 1) * kv_w]
        dil_p.append(tail.reshape(batch, keep, 2, HEADS_PER_GROUP, HEAD_DIM))

    dil_q_tabs_s = _dil_tables(pos_s, DIL_SCALE)
    dil_k_tabs_s = _dil_tables(pos_s, 1.0)
    bufs_t = []
    for buf in buffers:
        length = buf.shape[1]
        assert min(length, length + t_new) == length
        bufs_t.append(jnp.transpose(buf, (0, 2, 3, 4, 1)).reshape(bd, kv_w, length))
    new_rows_pad = SUBLANES
    x = xs
    kv_f32_s = _norm_rope(x, g_shared_kv, w_shared_b, dil_k_tabs_s, kv_rope_blocks)[0]
    kv_s = kv_f32_s.reshape(bd, t_new, N_GROUPS, kv_w)
    kv_new = jnp.pad(kv_s, ((0, 0), (0, new_rows_pad - t_new), (0, 0), (0, 0)))
    states_t = [None] * N_GROUPS
    for l in range(n_a, depth):
        x = ffn(x, l, 0)
        q = _norm_rope(x, g_layers[l, 1], w_dil_q_b[l - n_a], dil_q_tabs_s, q_rope_blocks)[0]
        qg = q.reshape(bd, t_new, N_GROUPS, GROUP_WIDTH)
        outs, lses = [], []
        for gi, dil in enumerate(dils):
            emit = l == n_a
            o, lse, st = _dil_sample_group(qg[:, :, gi], bufs_t[gi], kv_new[:, :, gi], dil, emit)
            if emit:
                states_t[gi] = st
            outs.append(o)
            lses.append(lse)
        x = _dil_out(outs, lses, w_dil_o_b[l - n_a], x)
        x = ffn(x, l, 1, final_norm=(l == depth - 1))
    y_sample = x.reshape(bd, t_new, d_model)
    mla_rows_sample = jnp.stack(rows_s, axis=0).reshape(n_a, bd, t_new, kv_lora + QK_ROPE)
    dil_s = []
    for gi, buf in enumerate(buffers):
        length = buf.shape[1]
        st = states_t[gi].reshape(bd, 2, HEADS_PER_GROUP, HEAD_DIM, length)
        dil_s.append(jnp.transpose(st, (0, 4, 1, 2, 3)))

    return (y_prompt, y_sample, mla_rows_prompt, mla_rows_sample,
            dil_p[0], dil_p[1], dil_p[2], dil_s[0], dil_s[1], dil_s[2])
```
